```python
import jax, jax.numpy as jnp
from jax import lax
import numpy as np

D_MODEL = 1024
BATCH = 4
SEQ = 4096
DEPTH = 1

CHUNK = 64
SSM_INNER = 1024
SSM_HEAD_DIM = 64
SSM_HEADS = SSM_INNER // SSM_HEAD_DIM
SSM_GROUPS = 2
SSM_STATE = 128
SSM_CONV = 4
SSM_CONV_DIM = SSM_INNER + 2 * SSM_GROUPS * SSM_STATE
LRU_WIDTH = 1024
LRU_BLOCKS = 16
LRU_BLOCK = LRU_WIDTH // LRU_BLOCKS
LRU_CONV = 4
LRU_C = 8.0
FFN_DIM = 3072
FFN_CONV = 3
N_BRANCHES = 2
RMS_EPS = 1e-6
IN_SPLITS = (
    SSM_INNER,
    SSM_INNER + SSM_CONV_DIM,
    SSM_INNER + SSM_CONV_DIM + SSM_HEADS,
    SSM_INNER + SSM_CONV_DIM + SSM_HEADS + LRU_WIDTH,
    SSM_INNER + SSM_CONV_DIM + SSM_HEADS + 2 * LRU_WIDTH,
)
N_IN = SSM_INNER + SSM_CONV_DIM + SSM_HEADS + 2 * LRU_WIDTH + N_BRANCHES * D_MODEL

kernel_name = "hybrid_ssd_rglru_gated_merge_convffn"


def rmsnorm(x, w):
    xf = x.astype(jnp.float32)
    y = xf * lax.rsqrt(jnp.mean(xf * xf, axis=-1, keepdims=True) + RMS_EPS)
    return (y * w.astype(jnp.float32)).astype(x.dtype)


def causal_dwconv(x, w, b):
    k = w.shape[0]
    out = lax.conv_general_dilated(
        x, w[:, None, :].astype(x.dtype), window_strides=(1,), padding=[(k - 1, 0)],
        dimension_numbers=("NWC", "WIO", "NWC"), feature_group_count=x.shape[-1])
    return out + b


def ssd_chunked(xh, dt, a, bm, cm):
    b, s = xh.shape[:2]
    c = s // CHUNK
    e = SSM_HEADS // SSM_GROUPS
    x = xh.reshape(b, c, CHUNK, SSM_GROUPS, e, SSM_HEAD_DIM)
    dtc = dt.reshape(b, c, CHUNK, SSM_GROUPS, e)
    bc = bm.reshape(b, c, CHUNK, SSM_GROUPS, SSM_STATE)
    cc = cm.reshape(b, c, CHUNK, SSM_GROUPS, SSM_STATE)
    a_dt = (dtc * a.reshape(SSM_GROUPS, e)).transpose(0, 3, 4, 1, 2)
    a_cs = jnp.cumsum(a_dt, axis=-1)
    xdt = x * dtc[..., None]
    causal = jnp.tril(jnp.ones((CHUNK, CHUNK), dtype=bool))
    seg = a_cs[..., :, None] - a_cs[..., None, :]
    decay = jnp.exp(jnp.where(causal, seg, -jnp.inf))
    cb = jnp.einsum("bclgn,bcsgn->bgcls", cc, bc)
    y_diag = jnp.einsum("bgcls,bgecls,bcsgep->bclgep", cb, decay, xdt)
    decay_states = jnp.exp(a_cs[..., -1:] - a_cs)
    states = jnp.einsum("bclgn,bgecl,bclgep->cbgepn", bc, decay_states, xdt)
    chunk_decay = jnp.exp(a_cs[..., -1]).transpose(3, 0, 1, 2)

    def step(h, inp):
        st, dec = inp
        return h * dec[..., None, None] + st, h

    h0 = jnp.zeros(states.shape[1:], states.dtype)
    _, prev = lax.scan(step, h0, (states, chunk_decay))
    y_off = jnp.einsum("bclgn,cbgepn,bgecl->bclgep", cc, prev, jnp.exp(a_cs))
    return (y_diag + y_off).reshape(b, s, SSM_HEADS, SSM_HEAD_DIM)


def ssd_mixer(u_z, u_xbc, u_dt, conv_w, conv_b, dt_bias, a_log, d_skip, norm_w):
    b, s = u_z.shape[:2]
    xbc = jax.nn.silu(causal_dwconv(u_xbc, conv_w, conv_b)).astype(jnp.float32)
    xs, bm, cm = jnp.split(xbc, [SSM_INNER, SSM_INNER + SSM_GROUPS * SSM_STATE], axis=-1)
    dt = jax.nn.softplus(u_dt.astype(jnp.float32) + dt_bias.astype(jnp.float32))
    a = -jnp.exp(a_log.astype(jnp.float32))
    xh = xs.reshape(b, s, SSM_HEADS, SSM_HEAD_DIM)
    y = ssd_chunked(xh, dt, a,
                    bm.reshape(b, s, SSM_GROUPS, SSM_STATE),
                    cm.reshape(b, s, SSM_GROUPS, SSM_STATE))
    y = y + d_skip.astype(jnp.float32)[:, None] * xh
    yg = (y.reshape(b, s, SSM_INNER) * jax.nn.silu(u_z.astype(jnp.float32)))
    yg = yg.reshape(b, s, SSM_GROUPS, SSM_INNER // SSM_GROUPS)
    yg = yg * lax.rsqrt(jnp.mean(yg * yg, axis=-1, keepdims=True) + RMS_EPS)
    yg = yg.reshape(b, s, SSM_INNER) * norm_w.astype(jnp.float32)
    return yg.astype(u_z.dtype)


def rglru_mixer(u_y, u_x, conv_w, conv_b, wr, br, wi, bi, lam):
    b, s = u_x.shape[:2]
    gate = jax.nn.gelu(u_y, approximate=True)
    xc = causal_dwconv(u_x, conv_w, conv_b)
    xb = xc.reshape(b, s, LRU_BLOCKS, LRU_BLOCK)
    r = jax.nn.sigmoid(jnp.einsum("bshi,hij->bshj", xb, wr) + br).reshape(b, s, LRU_WIDTH)
    i = jax.nn.sigmoid(jnp.einsum("bshi,hij->bshj", xb, wi) + bi).reshape(b, s, LRU_WIDTH)
    log_a = -LRU_C * r.astype(jnp.float32) * jax.nn.softplus(-lam.astype(jnp.float32))
    a = jnp.exp(log_a)
    mult = jnp.sqrt(-jnp.expm1(2.0 * log_a))
    bx = mult * i.astype(jnp.float32) * xc.astype(jnp.float32)

    def combine(lhs, rhs):
        a1, b1 = lhs
        a2, b2 = rhs
        return a1 * a2, a2 * b1 + b2

    _, h = lax.associative_scan(combine, (a, bx), axis=1)
    return (h.astype(u_x.dtype) * gate)


def setup_inputs(seed: int = 0) -> dict:
    key = jax.random.key(seed)
    ks = jax.random.split(key, 32)
    L = DEPTH
    f32 = jnp.float32

    def nrm(k, shape, scale):
        return jax.random.normal(k, shape, f32) * scale

    def gain(k, shape):
        return 1.0 + 0.05 * jax.random.normal(k, shape, f32)

    dt0 = jnp.exp(jax.random.uniform(ks[8], (L, SSM_HEADS), f32, np.log(1e-3), np.log(1e-1)))
    a_c = jax.random.uniform(ks[16], (L, LRU_WIDTH), f32, 0.9, 0.999)
    a0 = a_c ** (1.0 / LRU_C)
    return {
        "x": nrm(ks[0], (BATCH, SEQ, D_MODEL), 1.0),
        "mix_pre_norm": gain(ks[1], (L, D_MODEL)),
        "mix_post_norm": gain(ks[2], (L, D_MODEL)),
        "w_in": nrm(ks[3], (L, D_MODEL, N_IN), D_MODEL ** -0.5),
        "ssm_conv_w": nrm(ks[4], (L, SSM_CONV, SSM_CONV_DIM), SSM_CONV ** -0.5),
        "ssm_conv_b": nrm(ks[5], (L, SSM_CONV_DIM), 0.02),
        "ssm_dt_bias": dt0 + jnp.log(-jnp.expm1(-dt0)),
        "ssm_a_log": jnp.log(jax.random.uniform(ks[6], (L, SSM_HEADS), f32, 1.0, 16.0)),
        "ssm_d": 1.0 + 0.1 * jax.random.normal(ks[7], (L, SSM_HEADS), f32),
        "ssm_norm": gain(ks[9], (L, SSM_INNER)),
        "w_proj_ssm": nrm(ks[10], (L, SSM_INNER, D_MODEL), SSM_INNER ** -0.5),
        "lru_conv_w": nrm(ks[11], (L, LRU_CONV, LRU_WIDTH), LRU_CONV ** -0.5),
        "lru_conv_b": nrm(ks[12], (L, LRU_WIDTH), 0.02),
        "lru_wr": nrm(ks[13], (L, LRU_BLOCKS, LRU_BLOCK, LRU_BLOCK), LRU_BLOCK ** -0.5),
        "lru_br": nrm(ks[14], (L, LRU_BLOCKS, LRU_BLOCK), 0.02),
        "lru_wi": nrm(ks[15], (L, LRU_BLOCKS, LRU_BLOCK, LRU_BLOCK), LRU_BLOCK ** -0.5),
        "lru_bi": nrm(ks[17], (L, LRU_BLOCKS, LRU_BLOCK), 0.02),
        "lru_lambda": jnp.log(a0) - jnp.log1p(-a0),
        "w_proj_lru": nrm(ks[18], (L, LRU_WIDTH, D_MODEL), LRU_WIDTH ** -0.5),
        "gate_b": nrm(ks[19], (L, N_BRANCHES, D_MODEL), 0.02),
        "w_out": nrm(ks[20], (L, D_MODEL, D_MODEL), D_MODEL ** -0.5),
        "ffn_pre_norm": gain(ks[21], (L, D_MODEL)),
        "ffn_post_norm": gain(ks[22], (L, D_MODEL)),
        "w_ffn_up": nrm(ks[23], (L, D_MODEL, 2 * FFN_DIM), D_MODEL ** -0.5),
        "ffn_conv_w": nrm(ks[24], (L, FFN_CONV, 2 * FFN_DIM), FFN_CONV ** -0.5),
        "ffn_conv_b": nrm(ks[25], (L, 2 * FFN_DIM), 0.02),
        "w_ffn_down": nrm(ks[26], (L, FFN_DIM, D_MODEL), FFN_DIM ** -0.5),
    }


def reference(x, mix_pre_norm, mix_post_norm, w_in, ssm_conv_w, ssm_conv_b, ssm_dt_bias,
              ssm_a_log, ssm_d, ssm_norm, w_proj_ssm, lru_conv_w, lru_conv_b, lru_wr, lru_br,
              lru_wi, lru_bi, lru_lambda, w_proj_lru, gate_b, w_out, ffn_pre_norm,
              ffn_post_norm, w_ffn_up, ffn_conv_w, ffn_conv_b, w_ffn_down):
    b, s, _ = x.shape
    for l in range(DEPTH):
        h = rmsnorm(x, mix_pre_norm[l])
        u = h @ w_in[l]
        u_z, u_xbc, u_dt, u_ly, u_lx, u_g = jnp.split(u, list(IN_SPLITS), axis=-1)
        y_a = ssd_mixer(u_z, u_xbc, u_dt, ssm_conv_w[l], ssm_conv_b[l], ssm_dt_bias[l],
                        ssm_a_log[l], ssm_d[l], ssm_norm[l])
        y_b = rglru_mixer(u_ly, u_lx, lru_conv_w[l], lru_conv_b[l], lru_wr[l], lru_br[l],
                          lru_wi[l], lru_bi[l], lru_lambda[l])
        g = jax.nn.sigmoid(u_g.reshape(b, s, N_BRANCHES, D_MODEL) + gate_b[l])
        merged = g[:, :, 0, :] * (y_a @ w_proj_ssm[l]) + g[:, :, 1, :] * (y_b @ w_proj_lru[l])
        x = x + rmsnorm(merged @ w_out[l], mix_post_norm[l])
        h = rmsnorm(x, ffn_pre_norm[l])
        up = causal_dwconv(h @ w_ffn_up[l], ffn_conv_w[l], ffn_conv_b[l])
        f_gate, f_val = jnp.split(up, 2, axis=-1)
        f = (jax.nn.gelu(f_gate, approximate=True) * f_val) @ w_ffn_down[l]
        x = x + rmsnorm(f, ffn_post_norm[l])
    return x
```

```python
import functools

import jax
import jax.numpy as jnp
from jax import lax
from jax.experimental import pallas as pl
from jax.experimental.pallas import tpu as pltpu

F32 = jnp.float32
BF16 = jnp.bfloat16

D_MODEL = 1024
SSM_INNER = 1024
SSM_HEAD_DIM = 64
SSM_HEADS = 16
SSM_GROUPS = 2
SSM_STATE = 128
SSM_CONV = 4
SSM_CONV_DIM = SSM_INNER + 2 * SSM_GROUPS * SSM_STATE
LRU_WIDTH = 1024
LRU_BLOCKS = 16
LRU_BLOCK = 64
LRU_CONV = 4
LRU_C = 8.0
FFN_DIM = 3072
FFN_CONV = 3
RMS_EPS = 1e-6

LANES = 128
SUBLANES = 8
HEADS_PER_GROUP = SSM_HEADS // SSM_GROUPS
GROUP_WIDTH = SSM_INNER // SSM_GROUPS
HEAD_PAIRS = SSM_INNER // LANES

SEQ_TILE = 256
SSD_CHUNK = 128
VMEM_LIMIT = 56 * 1024 * 1024


def _rmsnorm(x, w):
    return x * lax.rsqrt(jnp.mean(x * x, axis=-1, keepdims=True) + RMS_EPS) * w


def _softplus(x):
    return jnp.maximum(x, 0.0) + jnp.log1p(jnp.exp(-jnp.abs(x)))


def _dot(a, b):
    return jnp.dot(a, b, preferred_element_type=F32)


def _causal_conv(buf_ref, x_new, w_ref, b_ref, taps, rows):
    buf_ref[SUBLANES:SUBLANES + rows, :] = x_new
    acc = b_ref[...] + w_ref[taps - 1:taps, :] * x_new
    for j in range(taps - 1):
        back = taps - 1 - j
        acc = acc + w_ref[j:j + 1, :] * buf_ref[SUBLANES - back:SUBLANES - back + rows, :]
    buf_ref[0:SUBLANES, :] = x_new[rows - SUBLANES:rows, :]
    return acc


def _expand_heads(v, expand_ref):
    lane = lax.broadcasted_iota(jnp.int32, v.shape, 1)
    v = jnp.where(lane < SSM_HEADS, v, 0.0)
    hi = v.astype(BF16).astype(F32)
    rem = v - hi
    mid = rem.astype(BF16).astype(F32)
    lo = rem - mid
    packed = hi + pltpu.roll(mid, SSM_HEADS, 1) + pltpu.roll(lo, 2 * SSM_HEADS, 1)
    return _dot(packed.astype(BF16), expand_ref[...])


def _mixer_kernel(x_ref, pre_w_ref, post_w_ref,
                  wz_ref, wxbc_ref, wdt_ref, wly_ref, wlx_ref, wg_ref,
                  sconv_w_ref, sconv_b_ref, dt_bias_ref, a_neg_ref, d_skip_ref, snorm_w_ref,
                  expand_ref, pa_ref,
                  lconv_w_ref, lconv_b_ref, wri_ref, br_ref, bi_ref, lam_ref, pb_ref,
                  gate_b_ref, wout_ref,
                  o_ref,
                  sconv_buf, lconv_buf, state_ref, lru_h_ref):
    ts = SEQ_TILE

    @pl.when(pl.program_id(1) == 0)
    def _():
        sconv_buf[0:SUBLANES, :] = jnp.zeros((SUBLANES, SSM_CONV_DIM), F32)
        lconv_buf[0:SUBLANES, :] = jnp.zeros((SUBLANES, LRU_WIDTH), F32)
        state_ref[...] = jnp.zeros_like(state_ref)
        lru_h_ref[...] = jnp.zeros_like(lru_h_ref)

    x = x_ref[...]
    hb = _rmsnorm(x, pre_w_ref[...]).astype(BF16)

    xbc = _causal_conv(sconv_buf, _dot(hb, wxbc_ref[...]), sconv_w_ref, sconv_b_ref, SSM_CONV, ts)
    xbc = xbc * jax.nn.sigmoid(xbc)
    xs = xbc[:, :SSM_INNER]
    bm = xbc[:, SSM_INNER:SSM_INNER + SSM_GROUPS * SSM_STATE]
    cm = xbc[:, SSM_INNER + SSM_GROUPS * SSM_STATE:]
    dt = _softplus(_dot(hb, wdt_ref[...]) + dt_bias_ref[...])
    a_dt = dt * a_neg_ref[...]

    lc = SSD_CHUNK
    row_i = lax.broadcasted_iota(jnp.int32, (lc, lc), 0)
    col_i = lax.broadcasted_iota(jnp.int32, (lc, lc), 1)
    causal = row_i >= col_i
    tril = causal.astype(F32)
    lane = lax.broadcasted_iota(jnp.int32, (lc, LANES), 1)
    lo_half = lane < SSM_HEAD_DIM

    y_chunks = []
    for c in range(ts // lc):
        rows = slice(c * lc, (c + 1) * lc)
        dt_c = dt[rows]
        a_cs = jnp.dot(tril, a_dt[rows], precision=lax.Precision.HIGHEST,
                       preferred_element_type=F32)
        a_cs_t = a_cs.T
        dt_t = dt_c.T
        a_last = a_cs[lc - 1:lc, :]
        e_in = jnp.exp(a_cs)
        w_out = jnp.exp(a_last - a_cs) * dt_c
        both = _expand_heads(jnp.concatenate([e_in, w_out], axis=0), expand_ref)
        e_full = both[:lc]
        w_full = both[lc:]
        xs_c = xs[rows]
        xw = (w_full * xs_c).astype(BF16)

        y_parts = []
        for g in range(SSM_GROUPS):
            gs = slice(g * GROUP_WIDTH, (g + 1) * GROUP_WIDTH)
            b_g = bm[rows, g * SSM_STATE:(g + 1) * SSM_STATE]
            c_g = cm[rows, g * SSM_STATE:(g + 1) * SSM_STATE].astype(BF16)
            cb = lax.dot_general(c_g, b_g.astype(BF16), (((1,), (1,)), ((), ())),
                                 preferred_element_type=F32)
            m_heads = []
            for e in range(HEADS_PER_GROUP):
                h = g * HEADS_PER_GROUP + e
                seg = a_cs[:, h:h + 1] - a_cs_t[h:h + 1, :]
                decay = jnp.exp(jnp.where(causal, seg, -jnp.inf))
                m_heads.append((decay * cb * dt_t[h:h + 1, :]).astype(BF16))
            yd = []
            for k in range(HEADS_PER_GROUP // 2):
                pair = g * (HEADS_PER_GROUP // 2) + k
                x_pair = xs_c[:, pair * LANES:(pair + 1) * LANES]
                rhs = jnp.concatenate([jnp.where(lo_half, x_pair, 0.0).astype(BF16),
                                       jnp.where(lo_half, 0.0, x_pair).astype(BF16)], axis=0)
                lhs = jnp.concatenate([m_heads[2 * k], m_heads[2 * k + 1]], axis=1)
                yd.append(_dot(lhs, rhs))
            y_diag = jnp.concatenate(yd, axis=1)
            st = state_ref[:, gs]
            y_off = e_full[:, gs] * _dot(c_g, st.astype(BF16))
            upd = _dot(b_g.T.astype(BF16), xw[:, gs])
            state_ref[:, gs] = e_full[lc - 1:lc, gs] * st + upd
            y_parts.append(y_diag + y_off)
        y_chunks.append(jnp.concatenate(y_parts, axis=1))
    y = jnp.concatenate(y_chunks, axis=0) + d_skip_ref[...] * xs

    z = _dot(hb, wz_ref[...])
    yg = y * (z * jax.nn.sigmoid(z))
    yn = []
    for g in range(SSM_GROUPS):
        part = yg[:, g * GROUP_WIDTH:(g + 1) * GROUP_WIDTH]
        yn.append(part * lax.rsqrt(jnp.mean(part * part, axis=-1, keepdims=True) + RMS_EPS))
    y_a = (jnp.concatenate(yn, axis=1) * snorm_w_ref[...]).astype(BF16)

    xc = _causal_conv(lconv_buf, _dot(hb, wlx_ref[...]), lconv_w_ref, lconv_b_ref, LRU_CONV, ts)
    ri = []
    for k in range(LRU_WIDTH // LANES):
        ri.append(_dot(xc[:, k * LANES:(k + 1) * LANES].astype(BF16), wri_ref[k]))
    r = jax.nn.sigmoid(jnp.concatenate([p[:, :LANES] for p in ri], axis=1) + br_ref[...])
    i_gate = jax.nn.sigmoid(jnp.concatenate([p[:, LANES:] for p in ri], axis=1) + bi_ref[...])
    log_a = (-LRU_C) * r * _softplus(-lam_ref[...])
    a = jnp.exp(log_a)
    mult = jnp.sqrt(-jnp.tanh(log_a) * (1.0 + a * a))
    b = mult * i_gate * xc
    row = lax.broadcasted_iota(jnp.int32, (ts, LRU_WIDTH), 0)
    step = 1
    while step < ts:
        keep = row >= step
        a_prev = pltpu.roll(a, step, 0)
        b_prev = pltpu.roll(b, step, 0)
        b = jnp.where(keep, a * b_prev + b, b)
        a = jnp.where(keep, a * a_prev, a)
        step *= 2
    h_lru = a * lru_h_ref[0:1, :] + b
    lru_h_ref[0:1, :] = h_lru[ts - 1:ts, :]
    y_b = (h_lru * jax.nn.gelu(_dot(hb, wly_ref[...]), approximate=True)).astype(BF16)

    gates = jax.nn.sigmoid(_dot(hb, wg_ref[...]) + gate_b_ref[...])
    merged = (gates[:, :D_MODEL] * _dot(y_a, pa_ref[...])
              + gates[:, D_MODEL:] * _dot(y_b, pb_ref[...]))
    out = _dot(merged.astype(BF16), wout_ref[...])
    o_ref[...] = x + _rmsnorm(out, post_w_ref[...])


def _ffn_kernel(x_ref, pre_w_ref, post_w_ref, wup_ref, conv_w_ref, conv_b_ref, wdown_ref,
                o_ref, conv_buf):
    ts = SEQ_TILE

    @pl.when(pl.program_id(1) == 0)
    def _():
        conv_buf[0:SUBLANES, :] = jnp.zeros((SUBLANES, 2 * FFN_DIM), F32)

    x = x_ref[...]
    hb = _rmsnorm(x, pre_w_ref[...]).astype(BF16)
    up = _causal_conv(conv_buf, _dot(hb, wup_ref[...]), conv_w_ref, conv_b_ref, FFN_CONV, ts)
    act = (jax.nn.gelu(up[:, :FFN_DIM], approximate=True) * up[:, FFN_DIM:]).astype(BF16)
    f = _dot(act, wdown_ref[...])
    o_ref[...] = x + _rmsnorm(f, post_w_ref[...])


def _const_spec(shape):
    zeros = (0,) * len(shape)
    return pl.BlockSpec(shape, lambda b, t: zeros, pipeline_mode=pl.Buffered(1))


def _tile_spec():
    return pl.BlockSpec((None, SEQ_TILE, D_MODEL), lambda b, t: (b, t, 0))


def _call(body, name, x, consts, scratch):
    batch, seq, _ = x.shape
    return pl.pallas_call(
        body,
        out_shape=jax.ShapeDtypeStruct(x.shape, F32),
        grid=(batch, seq // SEQ_TILE),
        in_specs=[_tile_spec()] + [_const_spec(c.shape) for c in consts],
        out_specs=_tile_spec(),
        scratch_shapes=scratch,
        compiler_params=pltpu.CompilerParams(
            dimension_semantics=("arbitrary", "arbitrary"),
            vmem_limit_bytes=VMEM_LIMIT),
        name=name,
    )(x, *consts)


def _row(v):
    return v.reshape(1, -1).astype(F32)


def _pad_lanes(v):
    return jnp.pad(v, [(0, 0)] * (v.ndim - 1) + [(0, LANES - v.shape[-1])])


def _pair_block_diag(w):
    w = w.reshape(LRU_BLOCKS // 2, 2, LRU_BLOCK, LRU_BLOCK)
    zero = jnp.zeros_like(w[:, 0])
    top = jnp.concatenate([w[:, 0], zero], axis=2)
    bot = jnp.concatenate([zero, w[:, 1]], axis=2)
    return jnp.concatenate([top, bot], axis=1)


def kernel(x, mix_pre_norm, mix_post_norm, w_in, ssm_conv_w, ssm_conv_b, ssm_dt_bias, ssm_a_log, ssm_d, ssm_norm, w_proj_ssm, lru_conv_w, lru_conv_b, lru_wr, lru_br, lru_wi, lru_bi, lru_lambda, w_proj_lru, gate_b, w_out, ffn_pre_norm, ffn_post_norm, w_ffn_up, ffn_conv_w, ffn_conv_b, w_ffn_down):
    assert x.shape[1] % SEQ_TILE == 0 and SEQ_TILE % SSD_CHUNK == 0
    assert mix_pre_norm.shape[0] == 1, "one layer"
    l = 0
    wi = w_in[l].astype(BF16)
    c0 = SSM_INNER
    c1 = c0 + SSM_CONV_DIM
    c2 = c1 + SSM_HEADS
    c3 = c2 + LRU_WIDTH
    c4 = c3 + LRU_WIDTH
    head_of_lane = jnp.arange(SSM_INNER) // SSM_HEAD_DIM
    expand = (jnp.arange(LANES)[:, None] % SSM_HEADS == head_of_lane[None, :]) & (
        jnp.arange(LANES)[:, None] < 3 * SSM_HEADS)
    wri = jnp.concatenate([_pair_block_diag(lru_wr[l]), _pair_block_diag(lru_wi[l])], axis=2)

    mixer_consts = [
        _row(mix_pre_norm[l]), _row(mix_post_norm[l]),
        wi[:, :c0], wi[:, c0:c1], _pad_lanes(wi[:, c1:c2]), wi[:, c2:c3], wi[:, c3:c4], wi[:, c4:],
        ssm_conv_w[l].astype(F32), _row(ssm_conv_b[l]),
        _pad_lanes(_row(ssm_dt_bias[l])), _pad_lanes(_row(-jnp.exp(ssm_a_log[l].astype(F32)))),
        _row(jnp.repeat(ssm_d[l], SSM_HEAD_DIM)), _row(ssm_norm[l]),
        expand.astype(BF16), w_proj_ssm[l].astype(BF16),
        lru_conv_w[l].astype(F32), _row(lru_conv_b[l]), wri.astype(BF16),
        _row(lru_br[l]), _row(lru_bi[l]), _row(lru_lambda[l]), w_proj_lru[l].astype(BF16),
        _row(gate_b[l]), w_out[l].astype(BF16),
    ]
    mixer_scratch = [
        pltpu.VMEM((SEQ_TILE + SUBLANES, SSM_CONV_DIM), F32),
        pltpu.VMEM((SEQ_TILE + SUBLANES, LRU_WIDTH), F32),
        pltpu.VMEM((SSM_STATE, SSM_INNER), F32),
        pltpu.VMEM((SUBLANES, LRU_WIDTH), F32),
    ]
    x1 = _call(_mixer_kernel, "mixer", x, mixer_consts, mixer_scratch)

    ffn_consts = [
        _row(ffn_pre_norm[l]), _row(ffn_post_norm[l]),
        w_ffn_up[l].astype(BF16), ffn_conv_w[l].astype(F32), _row(ffn_conv_b[l]),
        w_ffn_down[l].astype(BF16),
    ]
    ffn_scratch = [pltpu.VMEM((SEQ_TILE + SUBLANES, 2 * FFN_DIM), F32)]
    return _call(_ffn_kernel, "ffn", x1, ffn_consts, ffn_scratch)
```

```python
import functools

import jax
import jax.numpy as jnp
from jax import lax
from jax.experimental import pallas as pl
from jax.experimental.pallas import tpu as pltpu

F32 = jnp.float32
BF16 = jnp.bfloat16

D_MODEL = 1024
SSM_INNER = 1024
SSM_HEAD_DIM = 64
SSM_HEADS = 16
SSM_GROUPS = 2
SSM_STATE = 128
SSM_CONV = 4
SSM_CONV_DIM = SSM_INNER + 2 * SSM_GROUPS * SSM_STATE
LRU_WIDTH = 1024
LRU_BLOCKS = 16
LRU_BLOCK = 64
LRU_CONV = 4
LRU_C = 8.0
FFN_DIM = 3072
FFN_CONV = 3
RMS_EPS = 1e-6

LANES = 128
SUBLANES = 8
HEADS_PER_GROUP = SSM_HEADS // SSM_GROUPS
GROUP_WIDTH = SSM_INNER // SSM_GROUPS
HEAD_PAIRS = SSM_INNER // LANES

SEQ_TILE = 256
SSD_CHUNK = 128
SCAN_PAD = SEQ_TILE // 2
FFN_BLOCK = 256
VMEM_LIMIT = 56 * 1024 * 1024


def _rmsnorm(x, w):
    return x * lax.rsqrt(jnp.mean(x * x, axis=-1, keepdims=True) + RMS_EPS) * w


def _softplus(x):
    return jnp.maximum(x, 0.0) + jnp.log1p(jnp.exp(-jnp.abs(x)))


def _dot(a, b):
    return jnp.dot(a, b, preferred_element_type=F32)


def _causal_conv(buf_ref, x_new, w_ref, b_ref, col0, taps):
    rows = x_new.shape[0]
    outs = []
    for j in range(x_new.shape[1] // LANES):
        tile = col0 // LANES + j
        cols = slice(col0 + j * LANES, col0 + (j + 1) * LANES)
        xj = x_new[:, j * LANES:(j + 1) * LANES]
        buf_ref[tile, SUBLANES:SUBLANES + rows, :] = xj
        acc = b_ref[:, cols] + w_ref[taps - 1:taps, cols] * xj
        for t in range(taps - 1):
            back = taps - 1 - t
            acc = acc + w_ref[t:t + 1, cols] * buf_ref[tile, SUBLANES - back:SUBLANES - back + rows, :]
        buf_ref[tile, 0:SUBLANES, :] = xj[rows - SUBLANES:rows, :]
        outs.append(acc)
    return outs


def _expand_heads(v, expand_ref):
    lane = lax.broadcasted_iota(jnp.int32, v.shape, 1)
    v = jnp.where(lane < SSM_HEADS, v, 0.0)
    hi = v.astype(BF16).astype(F32)
    rem = v - hi
    mid = rem.astype(BF16).astype(F32)
    lo = rem - mid
    packed = hi + pltpu.roll(mid, SSM_HEADS, 1) + pltpu.roll(lo, 2 * SSM_HEADS, 1)
    return _dot(packed.astype(BF16), expand_ref[...])


def _mixer_kernel(x_ref, pre_w_ref, post_w_ref,
                  wz_ref, wxbc_ref, wdt_ref, wly_ref, wlx_ref, wg_ref,
                  sconv_w_ref, sconv_b_ref, dt_bias_ref, a_neg_ref, d_skip_ref, snorm_w_ref,
                  expand_ref, pa_ref,
                  lconv_w_ref, lconv_b_ref, wri_ref, br_ref, bi_ref, lam_ref, pb_ref,
                  gate_b_ref, wout_ref,
                  o_ref,
                  sconv_buf, lconv_buf, state_ref, lru_h_ref, scan_a, scan_b):
    ts = SEQ_TILE

    @pl.when(pl.program_id(1) == 0)
    def _():
        sconv_buf[:, 0:SUBLANES, :] = jnp.zeros((SSM_CONV_DIM // LANES, SUBLANES, LANES), F32)
        lconv_buf[:, 0:SUBLANES, :] = jnp.zeros((LRU_WIDTH // LANES, SUBLANES, LANES), F32)
        state_ref[...] = jnp.zeros_like(state_ref)
        lru_h_ref[...] = jnp.zeros_like(lru_h_ref)
        scan_a[:, 0:SCAN_PAD, :] = jnp.ones((LRU_WIDTH // LANES, SCAN_PAD, LANES), F32)
        scan_b[:, 0:SCAN_PAD, :] = jnp.zeros((LRU_WIDTH // LANES, SCAN_PAD, LANES), F32)

    x = x_ref[...]
    hb = _rmsnorm(x, pre_w_ref[...]).astype(BF16)

    xbc_tiles = _causal_conv(sconv_buf, _dot(hb, wxbc_ref[...]), sconv_w_ref, sconv_b_ref, 0, SSM_CONV)
    xbc = jnp.concatenate([v * jax.nn.sigmoid(v) for v in xbc_tiles], axis=1)
    xs = xbc[:, :SSM_INNER]
    bm = xbc[:, SSM_INNER:SSM_INNER + SSM_GROUPS * SSM_STATE]
    cm = xbc[:, SSM_INNER + SSM_GROUPS * SSM_STATE:]
    dt = _softplus(_dot(hb, wdt_ref[...]) + dt_bias_ref[...])
    a_dt = dt * a_neg_ref[...]

    lc = SSD_CHUNK
    row_i = lax.broadcasted_iota(jnp.int32, (lc, lc), 0)
    col_i = lax.broadcasted_iota(jnp.int32, (lc, lc), 1)
    causal = row_i >= col_i
    tril = causal.astype(F32)
    lane = lax.broadcasted_iota(jnp.int32, (lc, LANES), 1)
    lo_half = lane < SSM_HEAD_DIM

    y_chunks = []
    for c in range(ts // lc):
        rows = slice(c * lc, (c + 1) * lc)
        dt_c = dt[rows]
        a_cs = jnp.dot(tril, a_dt[rows], precision=lax.Precision.HIGHEST,
                       preferred_element_type=F32)
        a_cs_t = a_cs.T
        dt_t = dt_c.T
        a_last = a_cs[lc - 1:lc, :]
        e_in = jnp.exp(a_cs)
        w_out = jnp.exp(a_last - a_cs) * dt_c
        both = _expand_heads(jnp.concatenate([e_in, w_out], axis=0), expand_ref)
        e_full = both[:lc]
        w_full = both[lc:]
        xs_c = xs[rows]
        xw = (w_full * xs_c).astype(BF16)

        y_parts = []
        for g in range(SSM_GROUPS):
            gs = slice(g * GROUP_WIDTH, (g + 1) * GROUP_WIDTH)
            b_g = bm[rows, g * SSM_STATE:(g + 1) * SSM_STATE]
            c_g = cm[rows, g * SSM_STATE:(g + 1) * SSM_STATE].astype(BF16)
            cb = lax.dot_general(c_g, b_g.astype(BF16), (((1,), (1,)), ((), ())),
                                 preferred_element_type=F32)
            m_heads = []
            for e in range(HEADS_PER_GROUP):
                h = g * HEADS_PER_GROUP + e
                seg = a_cs[:, h:h + 1] - a_cs_t[h:h + 1, :]
                decay = jnp.exp(jnp.where(causal, seg, -jnp.inf))
                m_heads.append((decay * cb * dt_t[h:h + 1, :]).astype(BF16))
            yd = []
            for k in range(HEADS_PER_GROUP // 2):
                pair = g * (HEADS_PER_GROUP // 2) + k
                x_pair = xs_c[:, pair * LANES:(pair + 1) * LANES]
                rhs = jnp.concatenate([jnp.where(lo_half, x_pair, 0.0).astype(BF16),
                                       jnp.where(lo_half, 0.0, x_pair).astype(BF16)], axis=0)
                lhs = jnp.concatenate([m_heads[2 * k], m_heads[2 * k + 1]], axis=1)
                yd.append(_dot(lhs, rhs))
            y_diag = jnp.concatenate(yd, axis=1)
            st = state_ref[:, gs]
            y_off = e_full[:, gs] * _dot(c_g, st.astype(BF16))
            upd = _dot(b_g.T.astype(BF16), xw[:, gs])
            state_ref[:, gs] = e_full[lc - 1:lc, gs] * st + upd
            y_parts.append(y_diag + y_off)
        y_chunks.append(jnp.concatenate(y_parts, axis=1))
    y = jnp.concatenate(y_chunks, axis=0) + d_skip_ref[...] * xs

    z = _dot(hb, wz_ref[...])
    yg = y * (z * jax.nn.sigmoid(z))
    yn = []
    for g in range(SSM_GROUPS):
        part = yg[:, g * GROUP_WIDTH:(g + 1) * GROUP_WIDTH]
        yn.append(part * lax.rsqrt(jnp.mean(part * part, axis=-1, keepdims=True) + RMS_EPS))
    y_a = (jnp.concatenate(yn, axis=1) * snorm_w_ref[...]).astype(BF16)

    xc_tiles = _causal_conv(lconv_buf, _dot(hb, wlx_ref[...]), lconv_w_ref, lconv_b_ref, 0, LRU_CONV)
    ly = _dot(hb, wly_ref[...])
    neg_c_sp = (-LRU_C) * _softplus(-lam_ref[...])
    y_b_tiles = []
    for k, xc in enumerate(xc_tiles):
        cols = slice(k * LANES, (k + 1) * LANES)
        ri = _dot(xc.astype(BF16), wri_ref[k])
        r = jax.nn.sigmoid(ri[:, :LANES] + br_ref[:, cols])
        i_gate = jax.nn.sigmoid(ri[:, LANES:] + bi_ref[:, cols])
        log_a = r * neg_c_sp[:, cols]
        a = jnp.exp(log_a)
        mult = jnp.sqrt(-jnp.tanh(log_a) * (1.0 + a * a))
        b = mult * i_gate * xc
        step = 1
        while step < ts:
            scan_a[k, SCAN_PAD:SCAN_PAD + ts, :] = a
            scan_b[k, SCAN_PAD:SCAN_PAD + ts, :] = b
            b = a * scan_b[k, SCAN_PAD - step:SCAN_PAD - step + ts, :] + b
            a = a * scan_a[k, SCAN_PAD - step:SCAN_PAD - step + ts, :]
            step *= 2
        h_lru = a * lru_h_ref[0:1, cols] + b
        lru_h_ref[0:1, cols] = h_lru[ts - 1:ts, :]
        y_b_tiles.append(h_lru * jax.nn.gelu(ly[:, cols], approximate=True))
    y_b = jnp.concatenate(y_b_tiles, axis=1).astype(BF16)

    gates = jax.nn.sigmoid(_dot(hb, wg_ref[...]) + gate_b_ref[...])
    merged = (gates[:, :D_MODEL] * _dot(y_a, pa_ref[...])
              + gates[:, D_MODEL:] * _dot(y_b, pb_ref[...]))
    out = _dot(merged.astype(BF16), wout_ref[...])
    o_ref[...] = x + _rmsnorm(out, post_w_ref[...])


def _ffn_kernel(x_ref, pre_w_ref, post_w_ref, wup_ref, conv_w_ref, conv_b_ref, wdown_ref,
                o_ref, conv_buf, act_ref):
    @pl.when(pl.program_id(1) == 0)
    def _():
        conv_buf[:, 0:SUBLANES, :] = jnp.zeros((2 * FFN_DIM // LANES, SUBLANES, LANES), F32)

    x = x_ref[...]
    hb = _rmsnorm(x, pre_w_ref[...]).astype(BF16)
    tiles = FFN_BLOCK // LANES
    for jb in range(FFN_DIM // FFN_BLOCK):
        col0 = jb * 2 * FFN_BLOCK
        up = _causal_conv(conv_buf, _dot(hb, wup_ref[:, col0:col0 + 2 * FFN_BLOCK]),
                          conv_w_ref, conv_b_ref, col0, FFN_CONV)
        act = [jax.nn.gelu(up[j], approximate=True) * up[tiles + j] for j in range(tiles)]
        act_ref[:, jb * FFN_BLOCK:(jb + 1) * FFN_BLOCK] = jnp.concatenate(act, axis=1).astype(BF16)
    f = _dot(act_ref[...], wdown_ref[...])
    o_ref[...] = x + _rmsnorm(f, post_w_ref[...])


def _const_spec(shape):
    zeros = (0,) * len(shape)
    return pl.BlockSpec(shape, lambda b, t: zeros, pipeline_mode=pl.Buffered(1))


def _tile_spec():
    return pl.BlockSpec((None, SEQ_TILE, D_MODEL), lambda b, t: (b, t, 0))


def _call(body, name, x, consts, scratch):
    batch, seq, _ = x.shape
    return pl.pallas_call(
        body,
        out_shape=jax.ShapeDtypeStruct(x.shape, F32),
        grid=(batch, seq // SEQ_TILE),
        in_specs=[_tile_spec()] + [_const_spec(c.shape) for c in consts],
        out_specs=_tile_spec(),
        scratch_shapes=scratch,
        compiler_params=pltpu.CompilerParams(
            dimension_semantics=("arbitrary", "arbitrary"),
            vmem_limit_bytes=VMEM_LIMIT),
        name=name,
    )(x, *consts)


def _row(v):
    return v.reshape(1, -1).astype(F32)


def _pad_lanes(v):
    return jnp.pad(v, [(0, 0)] * (v.ndim - 1) + [(0, LANES - v.shape[-1])])


def _pair_block_diag(w):
    w = w.reshape(LRU_BLOCKS // 2, 2, LRU_BLOCK, LRU_BLOCK)
    zero = jnp.zeros_like(w[:, 0])
    top = jnp.concatenate([w[:, 0], zero], axis=2)
    bot = jnp.concatenate([zero, w[:, 1]], axis=2)
    return jnp.concatenate([top, bot], axis=1)


def _interleave_gate_value(w):
    lead = w.shape[:-1]
    w = w.reshape(lead + (2, FFN_DIM // FFN_BLOCK, FFN_BLOCK))
    return jnp.swapaxes(w, -3, -2).reshape(lead + (2 * FFN_DIM,))


def kernel(x, mix_pre_norm, mix_post_norm, w_in, ssm_conv_w, ssm_conv_b, ssm_dt_bias, ssm_a_log, ssm_d, ssm_norm, w_proj_ssm, lru_conv_w, lru_conv_b, lru_wr, lru_br, lru_wi, lru_bi, lru_lambda, w_proj_lru, gate_b, w_out, ffn_pre_norm, ffn_post_norm, w_ffn_up, ffn_conv_w, ffn_conv_b, w_ffn_down):
    assert x.shape[1] % SEQ_TILE == 0 and SEQ_TILE % SSD_CHUNK == 0
    assert mix_pre_norm.shape[0] == 1, "one layer"
    l = 0
    wi = w_in[l].astype(BF16)
    c0 = SSM_INNER
    c1 = c0 + SSM_CONV_DIM
    c2 = c1 + SSM_HEADS
    c3 = c2 + LRU_WIDTH
    c4 = c3 + LRU_WIDTH
    head_of_lane = jnp.arange(SSM_INNER) // SSM_HEAD_DIM
    expand = (jnp.arange(LANES)[:, None] % SSM_HEADS == head_of_lane[None, :]) & (
        jnp.arange(LANES)[:, None] < 3 * SSM_HEADS)
    wri = jnp.concatenate([_pair_block_diag(lru_wr[l]), _pair_block_diag(lru_wi[l])], axis=2)

    mixer_consts = [
        _row(mix_pre_norm[l]), _row(mix_post_norm[l]),
        wi[:, :c0], wi[:, c0:c1], _pad_lanes(wi[:, c1:c2]), wi[:, c2:c3], wi[:, c3:c4], wi[:, c4:],
        ssm_conv_w[l].astype(F32), _row(ssm_conv_b[l]),
        _pad_lanes(_row(ssm_dt_bias[l])), _pad_lanes(_row(-jnp.exp(ssm_a_log[l].astype(F32)))),
        _row(jnp.repeat(ssm_d[l], SSM_HEAD_DIM)), _row(ssm_norm[l]),
        expand.astype(BF16), w_proj_ssm[l].astype(BF16),
        lru_conv_w[l].astype(F32), _row(lru_conv_b[l]), wri.astype(BF16),
        _row(lru_br[l]), _row(lru_bi[l]), _row(lru_lambda[l]), w_proj_lru[l].astype(BF16),
        _row(gate_b[l]), w_out[l].astype(BF16),
    ]
    mixer_scratch = [
        pltpu.VMEM((SSM_CONV_DIM // LANES, SUBLANES + SEQ_TILE, LANES), F32),
        pltpu.VMEM((LRU_WIDTH // LANES, SUBLANES + SEQ_TILE, LANES), F32),
        pltpu.VMEM((SSM_STATE, SSM_INNER), F32),
        pltpu.VMEM((SUBLANES, LRU_WIDTH), F32),
        pltpu.VMEM((LRU_WIDTH // LANES, SCAN_PAD + SEQ_TILE, LANES), F32),
        pltpu.VMEM((LRU_WIDTH // LANES, SCAN_PAD + SEQ_TILE, LANES), F32),
    ]
    x1 = _call(_mixer_kernel, "mixer", x, mixer_consts, mixer_scratch)

    ffn_consts = [
        _row(ffn_pre_norm[l]), _row(ffn_post_norm[l]),
        _interleave_gate_value(w_ffn_up[l]).astype(BF16),
        _interleave_gate_value(ffn_conv_w[l]).astype(F32),
        _interleave_gate_value(_row(ffn_conv_b[l])),
        w_ffn_down[l].astype(BF16),
    ]
    ffn_scratch = [
        pltpu.VMEM((2 * FFN_DIM // LANES, SUBLANES + SEQ_TILE, LANES), F32),
        pltpu.VMEM((SEQ_TILE, FFN_DIM), BF16),
    ]
    return _call(_ffn_kernel, "ffn", x1, ffn_consts, ffn_scratch)
```

```python
import jax
import jax.numpy as jnp
import numpy as np
from jax import lax
from jax.experimental import pallas as pl
from jax.experimental.pallas import tpu as pltpu

F32 = jnp.float32
BF16 = jnp.bfloat16

D_MODEL = 1024
SSM_INNER = 1024
SSM_HEAD_DIM = 64
SSM_HEADS = 16
SSM_GROUPS = 2
SSM_STATE = 128
SSM_CONV = 4
SSM_CONV_DIM = SSM_INNER + 2 * SSM_GROUPS * SSM_STATE
LRU_WIDTH = 1024
LRU_BLOCKS = 16
LRU_BLOCK = 64
LRU_CONV = 4
LRU_C = 8.0
FFN_DIM = 3072
FFN_CONV = 3
RMS_EPS = 1e-6

LANES = 128
SUBLANES = 8
HEADS_PER_GROUP = SSM_HEADS // SSM_GROUPS
GROUP_WIDTH = SSM_INNER // SSM_GROUPS

IN_Z = 0
IN_XBC = IN_Z + SSM_INNER
IN_DT = IN_XBC + SSM_CONV_DIM
IN_LY = IN_DT + LANES
IN_LX = IN_LY + LRU_WIDTH
IN_G = IN_LX + LRU_WIDTH
IN_END = IN_G + 2 * D_MODEL

SEQ_TILE = 256
FFN_TILE = 512
SSD_CHUNK = 128
SCAN_BLOCK = 32
SCAN_PAD = SCAN_BLOCK // 2
SCAN_PITCH = SCAN_PAD + SCAN_BLOCK
CS_PAD = SSD_CHUNK // 2
PROJ_PIECE = 256
FFN_BLOCK = 512
VMEM_LIMIT = 56 * 1024 * 1024


def _rmsnorm(x, w):
    return x * lax.rsqrt(jnp.mean(x * x, axis=-1, keepdims=True) + RMS_EPS) * w


def _softplus(x):
    return jnp.maximum(x, 0.0) + jnp.log1p(jnp.exp(-jnp.abs(x)))


def _dot(a, b):
    return jnp.dot(a, b, preferred_element_type=F32)


def _causal_conv(buf_ref, x_new, w_ref, b_ref, col0, taps, after_tile=None):
    rows = x_new.shape[0]
    outs = []
    for j in range(x_new.shape[1] // LANES):
        tile = col0 // LANES + j
        cols = slice(col0 + j * LANES, col0 + (j + 1) * LANES)
        xj = x_new[:, j * LANES:(j + 1) * LANES]
        buf_ref[tile, SUBLANES:SUBLANES + rows, :] = xj
        acc = b_ref[:, cols] + w_ref[taps - 1:taps, cols] * xj
        for t in range(taps - 1):
            back = taps - 1 - t
            acc = acc + w_ref[t:t + 1, cols] * buf_ref[tile, SUBLANES - back:SUBLANES - back + rows, :]
        buf_ref[tile, 0:SUBLANES, :] = xj[rows - SUBLANES:rows, :]
        outs.append(acc)
        if after_tile is not None:
            after_tile(j)
    return outs


def _expand_heads(v, expand_ref):
    lane = lax.broadcasted_iota(jnp.int32, v.shape, 1)
    v = jnp.where(lane < SSM_HEADS, v, 0.0)
    hi = v.astype(BF16).astype(F32)
    rem = v - hi
    mid = rem.astype(BF16).astype(F32)
    lo = rem - mid
    packed = hi + pltpu.roll(mid, SSM_HEADS, 1) + pltpu.roll(lo, 2 * SSM_HEADS, 1)
    return _dot(packed.astype(BF16), expand_ref[...])


class _ProjectionQueue:
    def __init__(self, hb, win_ref):
        self._hb = hb
        self._win = win_ref
        self._pending = []
        self._done = {}

    def add(self, name, col0, width):
        for j in range(width // PROJ_PIECE):
            self._pending.append((name, j, col0 + j * PROJ_PIECE))

    def pump(self, count=1):
        for _ in range(min(count, len(self._pending))):
            name, j, c0 = self._pending.pop(0)
            self._done[name, j] = _dot(self._hb, self._win[:, c0:c0 + PROJ_PIECE])

    def tile(self, name, t):
        per_piece = PROJ_PIECE // LANES
        while (name, t // per_piece) not in self._done:
            self.pump()
        off = (t % per_piece) * LANES
        return self._done[name, t // per_piece][:, off:off + LANES]


def _mixer_kernel(x_ref, pre_w_ref, post_w_ref,
                  win_ref,
                  sconv_w_ref, sconv_b_ref, dt_bias_ref, a_log_ref, d_skip_ref, snorm_w_ref,
                  expand_ref, pa_ref,
                  lconv_w_ref, lconv_b_ref, wri_ref, br_ref, bi_ref, lam_ref, pb_ref,
                  gate_b_ref, wout_ref,
                  o_ref,
                  sconv_buf, lconv_buf, state_ref, lru_h_ref, scan_a, scan_b, cs_buf):
    ts = SEQ_TILE
    n_tiles = D_MODEL // LANES

    @pl.when(pl.program_id(1) == 0)
    def _():
        sconv_buf[:, 0:SUBLANES, :] = jnp.zeros((SSM_CONV_DIM // LANES, SUBLANES, LANES), F32)
        lconv_buf[:, 0:SUBLANES, :] = jnp.zeros((LRU_WIDTH // LANES, SUBLANES, LANES), F32)
        state_ref[...] = jnp.zeros_like(state_ref)
        lru_h_ref[...] = jnp.zeros_like(lru_h_ref)
        scan_a[...] = jnp.ones_like(scan_a)
        scan_b[...] = jnp.zeros_like(scan_b)
        cs_buf[...] = jnp.zeros_like(cs_buf)

    x = x_ref[...]
    hb = _rmsnorm(x, pre_w_ref[...]).astype(BF16)
    proj = _ProjectionQueue(hb, win_ref)
    proj.add("lx", IN_LX, LRU_WIDTH)
    proj.add("z", IN_Z, SSM_INNER)
    proj.add("ly", IN_LY, LRU_WIDTH)
    proj.add("g", IN_G, 2 * D_MODEL)

    xbc_tiles = _causal_conv(sconv_buf, _dot(hb, win_ref[:, IN_XBC:IN_DT]), sconv_w_ref, sconv_b_ref,
                             0, SSM_CONV, after_tile=lambda j: proj.pump() if j % 3 == 2 else None)
    xbc = jnp.concatenate([v * jax.nn.sigmoid(v) for v in xbc_tiles], axis=1)
    xs = xbc[:, :SSM_INNER]
    bm = xbc[:, SSM_INNER:SSM_INNER + SSM_GROUPS * SSM_STATE]
    cm = xbc[:, SSM_INNER + SSM_GROUPS * SSM_STATE:]
    dt = _softplus(_dot(hb, win_ref[:, IN_DT:IN_LY]) + dt_bias_ref[...])
    a_dt = dt * -jnp.exp(a_log_ref[...])

    lc = SSD_CHUNK
    row_i = lax.broadcasted_iota(jnp.int32, (lc, lc), 0)
    col_i = lax.broadcasted_iota(jnp.int32, (lc, lc), 1)
    causal = row_i >= col_i
    lane = lax.broadcasted_iota(jnp.int32, (lc, LANES), 1)
    lo_half = lane < SSM_HEAD_DIM

    y_chunks = []
    for c in range(ts // lc):
        rows = slice(c * lc, (c + 1) * lc)
        dt_c = dt[rows]
        a_cs = a_dt[rows]
        step = 1
        while step < lc:
            cs_buf[c, CS_PAD:CS_PAD + lc, :] = a_cs
            a_cs = a_cs + cs_buf[c, CS_PAD - step:CS_PAD - step + lc, :]
            step *= 2
        a_cs_t = a_cs.T
        dt_t = dt_c.T
        a_last = a_cs[lc - 1:lc, :]
        e_in = jnp.exp(a_cs)
        w_out = jnp.exp(a_last - a_cs) * dt_c
        both = _expand_heads(jnp.concatenate([e_in, w_out], axis=0), expand_ref)
        e_full = both[:lc]
        w_full = both[lc:]
        xs_c = xs[rows]
        xw = (w_full * xs_c).astype(BF16)

        y_parts = []
        for g in range(SSM_GROUPS):
            gs = slice(g * GROUP_WIDTH, (g + 1) * GROUP_WIDTH)
            b_g = bm[rows, g * SSM_STATE:(g + 1) * SSM_STATE]
            c_g = cm[rows, g * SSM_STATE:(g + 1) * SSM_STATE].astype(BF16)
            cb = lax.dot_general(c_g, b_g.astype(BF16), (((1,), (1,)), ((), ())),
                                 preferred_element_type=F32)
            yd = []
            for k in range(HEADS_PER_GROUP // 2):
                pair = g * (HEADS_PER_GROUP // 2) + k
                m_pair = []
                for h in (2 * pair, 2 * pair + 1):
                    seg = a_cs[:, h:h + 1] - a_cs_t[h:h + 1, :]
                    decay = jnp.exp(jnp.where(causal, seg, -jnp.inf))
                    m_pair.append((decay * cb * dt_t[h:h + 1, :]).astype(BF16))
                x_pair = xs_c[:, pair * LANES:(pair + 1) * LANES]
                rhs = jnp.concatenate([jnp.where(lo_half, x_pair, 0.0).astype(BF16),
                                       jnp.where(lo_half, 0.0, x_pair).astype(BF16)], axis=0)
                yd.append(_dot(jnp.concatenate(m_pair, axis=1), rhs))
                if (pair + c) % 2 == 0:
                    proj.pump()
            y_diag = jnp.concatenate(yd, axis=1)
            st = state_ref[:, gs]
            y_off = e_full[:, gs] * _dot(c_g, st.astype(BF16))
            upd = _dot(b_g.T.astype(BF16), xw[:, gs])
            state_ref[:, gs] = e_full[lc - 1:lc, gs] * st + upd
            y_parts.append(y_diag + y_off)
        y_chunks.append(jnp.concatenate(y_parts, axis=1))
    y = jnp.concatenate(y_chunks, axis=0) + d_skip_ref[...] * xs

    z = jnp.concatenate([proj.tile("z", t) for t in range(n_tiles)], axis=1)
    yg = y * (z * jax.nn.sigmoid(z))
    yn = []
    for g in range(SSM_GROUPS):
        part = yg[:, g * GROUP_WIDTH:(g + 1) * GROUP_WIDTH]
        yn.append(part * lax.rsqrt(jnp.mean(part * part, axis=-1, keepdims=True) + RMS_EPS))
    y_a = (jnp.concatenate(yn, axis=1) * snorm_w_ref[...]).astype(BF16)

    lx = jnp.concatenate([proj.tile("lx", t) for t in range(n_tiles)], axis=1)
    xc_tiles = _causal_conv(lconv_buf, lx, lconv_w_ref, lconv_b_ref, 0, LRU_CONV,
                            after_tile=lambda j: proj.pump() if j % 4 == 3 else None)
    neg_c_sp = (-LRU_C) * _softplus(-lam_ref[...])
    proj_a = []
    proj_b = None
    y_b_pair = []
    for k, xc in enumerate(xc_tiles):
        cols = slice(k * LANES, (k + 1) * LANES)
        ri = _dot(xc.astype(BF16), wri_ref[k])
        r = jax.nn.sigmoid(ri[:, :LANES] + br_ref[:, cols])
        i_gate = jax.nn.sigmoid(ri[:, LANES:] + bi_ref[:, cols])
        log_a = r * neg_c_sp[:, cols]
        a = jnp.exp(log_a)
        mult = jnp.sqrt(-jnp.tanh(log_a) * (1.0 + a * a))
        b = mult * i_gate * xc
        a_blk = [a[i * SCAN_BLOCK:(i + 1) * SCAN_BLOCK] for i in range(ts // SCAN_BLOCK)]
        b_blk = [b[i * SCAN_BLOCK:(i + 1) * SCAN_BLOCK] for i in range(ts // SCAN_BLOCK)]
        step = 1
        while step < SCAN_BLOCK:
            for i in range(len(a_blk)):
                r0 = i * SCAN_PITCH + SCAN_PAD
                scan_a[k, r0:r0 + SCAN_BLOCK, :] = a_blk[i]
                scan_b[k, r0:r0 + SCAN_BLOCK, :] = b_blk[i]
                b_blk[i] = a_blk[i] * scan_b[k, r0 - step:r0 - step + SCAN_BLOCK, :] + b_blk[i]
                a_blk[i] = a_blk[i] * scan_a[k, r0 - step:r0 - step + SCAN_BLOCK, :]
            step *= 2
        h_prev = lru_h_ref[0:1, cols]
        h_blk = []
        for a_i, b_i in zip(a_blk, b_blk):
            h_i = a_i * h_prev + b_i
            h_prev = h_i[SCAN_BLOCK - 1:SCAN_BLOCK, :]
            h_blk.append(h_i)
        lru_h_ref[0:1, cols] = h_prev
        h_lru = jnp.concatenate(h_blk, axis=0)
        y_b_pair.append((h_lru * jax.nn.gelu(proj.tile("ly", k), approximate=True)).astype(BF16))
        proj.pump()
        if k % 2 == 1:
            p0 = (k // 2) * PROJ_PIECE
            proj_a.append(_dot(y_a, pa_ref[:, p0:p0 + PROJ_PIECE]))
            part = _dot(jnp.concatenate(y_b_pair, axis=1), pb_ref[p0:p0 + PROJ_PIECE, :])
            proj_b = part if proj_b is None else proj_b + part
            y_b_pair = []

    g_pre = jnp.concatenate([proj.tile("g", t) for t in range(2 * n_tiles)], axis=1)
    gates = jax.nn.sigmoid(g_pre + gate_b_ref[...])
    merged = gates[:, :D_MODEL] * jnp.concatenate(proj_a, axis=1) + gates[:, D_MODEL:] * proj_b
    out = _dot(merged.astype(BF16), wout_ref[...])
    o_ref[...] = x + _rmsnorm(out, post_w_ref[...])


def _ffn_kernel(x_ref, pre_w_ref, post_w_ref, wup_ref, conv_w_ref, conv_b_ref, wdown_ref,
                o_ref, conv_buf, act_ref):

    @pl.when(pl.program_id(1) == 0)
    def _():
        conv_buf[:, 0:SUBLANES, :] = jnp.zeros((2 * FFN_DIM // LANES, SUBLANES, LANES), F32)

    x = x_ref[...]
    hb = _rmsnorm(x, pre_w_ref[...]).astype(BF16)
    for jb in range(FFN_DIM // FFN_BLOCK):
        g0 = jb * FFN_BLOCK
        v0 = FFN_DIM + jb * FFN_BLOCK
        gate = _causal_conv(conv_buf, _dot(hb, wup_ref[:, g0:g0 + FFN_BLOCK]),
                            conv_w_ref, conv_b_ref, g0, FFN_CONV)
        val = _causal_conv(conv_buf, _dot(hb, wup_ref[:, v0:v0 + FFN_BLOCK]),
                           conv_w_ref, conv_b_ref, v0, FFN_CONV)
        act = [jax.nn.gelu(g, approximate=True) * v for g, v in zip(gate, val)]
        act_ref[:, g0:g0 + FFN_BLOCK] = jnp.concatenate(act, axis=1).astype(BF16)
    f = _dot(act_ref[...], wdown_ref[...])
    o_ref[...] = x + _rmsnorm(f, post_w_ref[...])


def _const_spec(shape):
    zeros = (0,) * len(shape)
    return pl.BlockSpec(shape, lambda b, t: zeros, pipeline_mode=pl.Buffered(1))


def _tile_spec(tile):
    return pl.BlockSpec((None, tile, D_MODEL), lambda b, t: (b, t, 0))


def _call(body, name, tile, x, consts, scratch):
    batch, seq, _ = x.shape
    assert seq % tile == 0
    return pl.pallas_call(
        body,
        out_shape=jax.ShapeDtypeStruct(x.shape, F32),
        grid=(batch, seq // tile),
        in_specs=[_tile_spec(tile)] + [_const_spec(c.shape) for c in consts],
        out_specs=_tile_spec(tile),
        scratch_shapes=scratch,
        compiler_params=pltpu.CompilerParams(
            dimension_semantics=("arbitrary", "arbitrary"),
            vmem_limit_bytes=VMEM_LIMIT),
        name=name,
    )(x, *consts)


def _row(v):
    return v.reshape(1, -1).astype(F32)


def _pad_lanes(v):
    return jnp.pad(v, [(0, 0)] * (v.ndim - 1) + [(0, LANES - v.shape[-1])])


def _pair_block_diag(w):
    w = w.reshape(LRU_BLOCKS // 2, 2, LRU_BLOCK, LRU_BLOCK)
    zero = jnp.zeros_like(w[:, 0])
    top = jnp.concatenate([w[:, 0], zero], axis=2)
    bot = jnp.concatenate([zero, w[:, 1]], axis=2)
    return jnp.concatenate([top, bot], axis=1)


def kernel(x, mix_pre_norm, mix_post_norm, w_in, ssm_conv_w, ssm_conv_b, ssm_dt_bias, ssm_a_log, ssm_d, ssm_norm, w_proj_ssm, lru_conv_w, lru_conv_b, lru_wr, lru_br, lru_wi, lru_bi, lru_lambda, w_proj_lru, gate_b, w_out, ffn_pre_norm, ffn_post_norm, w_ffn_up, ffn_conv_w, ffn_conv_b, w_ffn_down):
    assert SEQ_TILE % SSD_CHUNK == 0
    assert mix_pre_norm.shape[0] == 1, "one layer"
    l = 0
    dt_end = IN_DT + SSM_HEADS
    win = jnp.concatenate(
        [w_in[l][:, :dt_end].astype(BF16),
         jnp.zeros((D_MODEL, LANES - SSM_HEADS), BF16),
         w_in[l][:, dt_end:].astype(BF16)], axis=1)
    head_of_lane = np.arange(SSM_INNER) // SSM_HEAD_DIM
    piece_row = np.arange(LANES)[:, None]
    expand = (piece_row % SSM_HEADS == head_of_lane[None, :]) & (piece_row < 3 * SSM_HEADS)
    wri = jnp.concatenate([_pair_block_diag(lru_wr[l]), _pair_block_diag(lru_wi[l])], axis=2)

    mixer_consts = [
        _row(mix_pre_norm[l]), _row(mix_post_norm[l]),
        win,
        ssm_conv_w[l].astype(F32), _row(ssm_conv_b[l]),
        _pad_lanes(_row(ssm_dt_bias[l])), _pad_lanes(_row(ssm_a_log[l])),
        _row(jnp.repeat(ssm_d[l], SSM_HEAD_DIM)), _row(ssm_norm[l]),
        jnp.asarray(expand, BF16), w_proj_ssm[l].astype(BF16),
        lru_conv_w[l].astype(F32), _row(lru_conv_b[l]), wri.astype(BF16),
        _row(lru_br[l]), _row(lru_bi[l]), _row(lru_lambda[l]), w_proj_lru[l].astype(BF16),
        _row(gate_b[l]), w_out[l].astype(BF16),
    ]
    mixer_scratch = [
        pltpu.VMEM((SSM_CONV_DIM // LANES, SUBLANES + SEQ_TILE, LANES), F32),
        pltpu.VMEM((LRU_WIDTH // LANES, SUBLANES + SEQ_TILE, LANES), F32),
        pltpu.VMEM((SSM_STATE, SSM_INNER), F32),
        pltpu.VMEM((SUBLANES, LRU_WIDTH), F32),
        pltpu.VMEM((LRU_WIDTH // LANES, SEQ_TILE // SCAN_BLOCK * SCAN_PITCH, LANES), F32),
        pltpu.VMEM((LRU_WIDTH // LANES, SEQ_TILE // SCAN_BLOCK * SCAN_PITCH, LANES), F32),
        pltpu.VMEM((SEQ_TILE // SSD_CHUNK, CS_PAD + SSD_CHUNK, LANES), F32),
    ]
    x1 = _call(_mixer_kernel, "mixer", SEQ_TILE, x, mixer_consts, mixer_scratch)

    ffn_consts = [
        _row(ffn_pre_norm[l]), _row(ffn_post_norm[l]),
        w_ffn_up[l].astype(BF16), ffn_conv_w[l].astype(F32), _row(ffn_conv_b[l]),
        w_ffn_down[l].astype(BF16),
    ]
    ffn_scratch = [
        pltpu.VMEM((2 * FFN_DIM // LANES, SUBLANES + FFN_TILE, LANES), F32),
        pltpu.VMEM((FFN_TILE, FFN_DIM), BF16),
    ]
    return _call(_ffn_kernel, "ffn", FFN_TILE, x1, ffn_consts, ffn_scratch)
```

```python
import jax
import jax.numpy as jnp
import numpy as np
from jax import lax
from jax.experimental import pallas as pl
from jax.experimental.pallas import tpu as pltpu

F32 = jnp.float32
BF16 = jnp.bfloat16

D_MODEL = 1024
SSM_INNER = 1024
SSM_HEAD_DIM = 64
SSM_HEADS = 16
SSM_GROUPS = 2
SSM_STATE = 128
SSM_CONV = 4
SSM_CONV_DIM = SSM_INNER + 2 * SSM_GROUPS * SSM_STATE
LRU_WIDTH = 1024
LRU_BLOCKS = 16
LRU_BLOCK = 64
LRU_CONV = 4
LRU_C = 8.0
FFN_DIM = 3072
FFN_CONV = 3
RMS_EPS = 1e-6

LANES = 128
SUBLANES = 8
HEADS_PER_GROUP = SSM_HEADS // SSM_GROUPS
GROUP_WIDTH = SSM_INNER // SSM_GROUPS

A_Z = 0
A_XBC = A_Z + SSM_INNER
A_DT = A_XBC + SSM_CONV_DIM
A_END = A_DT + LANES
W_IN_B0 = A_DT + SSM_HEADS
B_LY = 0
B_LX = B_LY + LRU_WIDTH
B_G = B_LX + LRU_WIDTH
B_END = B_G + 2 * D_MODEL

MIXER_STREAMS = 2
STREAM_LEAD = 24
N_MIXER_SCRATCH = 7
SEQ_TILE = 256
FFN_TILE = 512
SSD_CHUNK = 128
SCAN_BLOCK = 32
SCAN_PAD = SCAN_BLOCK // 2
SCAN_PITCH = SCAN_PAD + SCAN_BLOCK
CS_PAD = SSD_CHUNK // 2
PROJ_PIECE = 256
FFN_BLOCK = 512
VMEM_LIMIT = 56 * 1024 * 1024


def _rmsnorm(x, w):
    return x * lax.rsqrt(jnp.mean(x * x, axis=-1, keepdims=True) + RMS_EPS) * w


def _softplus(x):
    return jnp.maximum(x, 0.0) + jnp.log1p(jnp.exp(-jnp.abs(x)))


def _dot(a, b):
    return jnp.dot(a, b, preferred_element_type=F32)


def _causal_conv(buf_ref, x_new, w_ref, b_ref, col0, taps, after_tile=None):
    rows = x_new.shape[0]
    outs = []
    for j in range(x_new.shape[1] // LANES):
        tile = col0 // LANES + j
        cols = slice(col0 + j * LANES, col0 + (j + 1) * LANES)
        xj = x_new[:, j * LANES:(j + 1) * LANES]
        buf_ref[tile, SUBLANES:SUBLANES + rows, :] = xj
        acc = b_ref[:, cols] + w_ref[taps - 1:taps, cols] * xj
        for t in range(taps - 1):
            back = taps - 1 - t
            acc = acc + w_ref[t:t + 1, cols] * buf_ref[tile, SUBLANES - back:SUBLANES - back + rows, :]
        buf_ref[tile, 0:SUBLANES, :] = xj[rows - SUBLANES:rows, :]
        outs.append(acc)
        if after_tile is not None:
            after_tile(j)
    return outs


def _expand_heads(v, expand_ref):
    lane = lax.broadcasted_iota(jnp.int32, v.shape, 1)
    v = jnp.where(lane < SSM_HEADS, v, 0.0)
    hi = v.astype(BF16).astype(F32)
    rem = v - hi
    mid = rem.astype(BF16).astype(F32)
    lo = rem - mid
    packed = hi + pltpu.roll(mid, SSM_HEADS, 1) + pltpu.roll(lo, 2 * SSM_HEADS, 1)
    return _dot(packed.astype(BF16), expand_ref[...])


class _ProjectionQueue:
    def __init__(self, hb):
        self._hb = hb
        self._pending = []
        self._done = {}

    def add(self, name, w_ref, col0, width):
        for j in range(width // PROJ_PIECE):
            self._pending.append((name, j, w_ref, col0 + j * PROJ_PIECE))

    def pump(self, count=1):
        for _ in range(min(count, len(self._pending))):
            name, j, w_ref, c0 = self._pending.pop(0)
            self._done[name, j] = _dot(self._hb, w_ref[:, c0:c0 + PROJ_PIECE])

    def tile(self, name, t):
        per_piece = PROJ_PIECE // LANES
        while (name, t // per_piece) not in self._done:
            self.pump()
        off = (t % per_piece) * LANES
        return self._done[name, t // per_piece][:, off:off + LANES]


def _run_interleaved(streams, lead):
    live = list(streams)
    for _ in range(lead):
        next(live[0], None)
    while live:
        live = [g for g in live if next(g, _DONE) is not _DONE]


_DONE = object()


def _mixer_kernel(x_ref, *refs):
    n_consts = len(refs) - 1 - N_MIXER_SCRATCH
    consts = refs[:n_consts]
    o_ref = refs[n_consts]
    scratch = refs[n_consts + 1:]
    sconv_buf, lconv_buf, state_ref, lru_h_ref, scan_a, scan_b, cs_buf = scratch

    @pl.when(pl.program_id(1) == 0)
    def _():
        sconv_buf[:, :, 0:SUBLANES, :] = jnp.zeros((MIXER_STREAMS, SSM_CONV_DIM // LANES, SUBLANES, LANES), F32)
        lconv_buf[:, :, 0:SUBLANES, :] = jnp.zeros((MIXER_STREAMS, LRU_WIDTH // LANES, SUBLANES, LANES), F32)
        state_ref[...] = jnp.zeros_like(state_ref)
        lru_h_ref[...] = jnp.zeros_like(lru_h_ref)
        scan_a[...] = jnp.ones_like(scan_a)
        scan_b[...] = jnp.zeros_like(scan_b)
        cs_buf[...] = jnp.zeros_like(cs_buf)

    streams = [_mixer_stream(x_ref.at[s], o_ref.at[s], consts, [r.at[s] for r in scratch])
               for s in range(MIXER_STREAMS)]
    _run_interleaved(streams, STREAM_LEAD)


def _mixer_stream(x_ref, o_ref, consts, scratch):
    (pre_w_ref, post_w_ref, wa_ref, wb_ref,
     sconv_w_ref, sconv_b_ref, dt_bias_ref, a_log_ref, d_skip_ref, snorm_w_ref,
     expand_ref, pa_ref,
     lconv_w_ref, lconv_b_ref, wri_ref, br_ref, bi_ref, lam_ref, pb_ref,
     gate_b_ref, wout_ref) = consts
    sconv_buf, lconv_buf, state_ref, lru_h_ref, scan_a, scan_b, cs_buf = scratch
    ts = SEQ_TILE
    n_tiles = D_MODEL // LANES

    x = x_ref[...]
    hb = _rmsnorm(x, pre_w_ref[...]).astype(BF16)
    yield
    proj = _ProjectionQueue(hb)
    proj.add("lx", wb_ref, B_LX, LRU_WIDTH)
    proj.add("z", wa_ref, A_Z, SSM_INNER)
    proj.add("ly", wb_ref, B_LY, LRU_WIDTH)
    proj.add("g", wb_ref, B_G, 2 * D_MODEL)

    xbc_tiles = _causal_conv(sconv_buf, _dot(hb, wa_ref[:, A_XBC:A_DT]), sconv_w_ref, sconv_b_ref,
                             0, SSM_CONV, after_tile=lambda j: proj.pump() if j % 3 == 2 else None)
    xbc = jnp.concatenate([v * jax.nn.sigmoid(v) for v in xbc_tiles], axis=1)
    xs = xbc[:, :SSM_INNER]
    bm = xbc[:, SSM_INNER:SSM_INNER + SSM_GROUPS * SSM_STATE]
    cm = xbc[:, SSM_INNER + SSM_GROUPS * SSM_STATE:]
    yield
    dt = _softplus(_dot(hb, wa_ref[:, A_DT:A_END]) + dt_bias_ref[...])
    a_dt = dt * -jnp.exp(a_log_ref[...])

    lc = SSD_CHUNK
    row_i = lax.broadcasted_iota(jnp.int32, (lc, lc), 0)
    col_i = lax.broadcasted_iota(jnp.int32, (lc, lc), 1)
    causal = row_i >= col_i
    lane = lax.broadcasted_iota(jnp.int32, (lc, LANES), 1)
    lo_half = lane < SSM_HEAD_DIM

    y_chunks = []
    for c in range(ts // lc):
        rows = slice(c * lc, (c + 1) * lc)
        dt_c = dt[rows]
        a_cs = a_dt[rows]
        step = 1
        while step < lc:
            cs_buf[c, CS_PAD:CS_PAD + lc, :] = a_cs
            a_cs = a_cs + cs_buf[c, CS_PAD - step:CS_PAD - step + lc, :]
            step *= 2
        a_cs_t = a_cs.T
        dt_t = dt_c.T
        a_last = a_cs[lc - 1:lc, :]
        e_in = jnp.exp(a_cs)
        w_out = jnp.exp(a_last - a_cs) * dt_c
        both = _expand_heads(jnp.concatenate([e_in, w_out], axis=0), expand_ref)
        e_full = both[:lc]
        w_full = both[lc:]
        xs_c = xs[rows]
        xw = (w_full * xs_c).astype(BF16)
        yield

        y_parts = []
        for g in range(SSM_GROUPS):
            gs = slice(g * GROUP_WIDTH, (g + 1) * GROUP_WIDTH)
            b_g = bm[rows, g * SSM_STATE:(g + 1) * SSM_STATE]
            c_g = cm[rows, g * SSM_STATE:(g + 1) * SSM_STATE].astype(BF16)
            cb = lax.dot_general(c_g, b_g.astype(BF16), (((1,), (1,)), ((), ())),
                                 preferred_element_type=F32)
            yd = []
            for k in range(HEADS_PER_GROUP // 2):
                pair = g * (HEADS_PER_GROUP // 2) + k
                m_pair = []
                for h in (2 * pair, 2 * pair + 1):
                    seg = a_cs[:, h:h + 1] - a_cs_t[h:h + 1, :]
                    decay = jnp.exp(jnp.where(causal, seg, -jnp.inf))
                    m_pair.append((decay * cb * dt_t[h:h + 1, :]).astype(BF16))
                x_pair = xs_c[:, pair * LANES:(pair + 1) * LANES]
                rhs = jnp.concatenate([jnp.where(lo_half, x_pair, 0.0).astype(BF16),
                                       jnp.where(lo_half, 0.0, x_pair).astype(BF16)], axis=0)
                yd.append(_dot(jnp.concatenate(m_pair, axis=1), rhs))
                if (pair + c) % 2 == 0:
                    proj.pump()
                yield
            y_diag = jnp.concatenate(yd, axis=1)
            st = state_ref[:, gs]
            y_off = e_full[:, gs] * _dot(c_g, st.astype(BF16))
            upd = _dot(b_g.T.astype(BF16), xw[:, gs])
            state_ref[:, gs] = e_full[lc - 1:lc, gs] * st + upd
            y_parts.append(y_diag + y_off)
            yield
        y_chunks.append(jnp.concatenate(y_parts, axis=1))
    y = jnp.concatenate(y_chunks, axis=0) + d_skip_ref[...] * xs

    z = jnp.concatenate([proj.tile("z", t) for t in range(n_tiles)], axis=1)
    yg = y * (z * jax.nn.sigmoid(z))
    yn = []
    for g in range(SSM_GROUPS):
        part = yg[:, g * GROUP_WIDTH:(g + 1) * GROUP_WIDTH]
        yn.append(part * lax.rsqrt(jnp.mean(part * part, axis=-1, keepdims=True) + RMS_EPS))
    y_a = (jnp.concatenate(yn, axis=1) * snorm_w_ref[...]).astype(BF16)
    yield

    lx = jnp.concatenate([proj.tile("lx", t) for t in range(n_tiles)], axis=1)
    xc_tiles = _causal_conv(lconv_buf, lx, lconv_w_ref, lconv_b_ref, 0, LRU_CONV,
                            after_tile=lambda j: proj.pump() if j % 4 == 3 else None)
    neg_c_sp = (-LRU_C) * _softplus(-lam_ref[...])
    yield
    proj_a = []
    proj_b = None
    y_b_pair = []
    for k, xc in enumerate(xc_tiles):
        cols = slice(k * LANES, (k + 1) * LANES)
        ri = _dot(xc.astype(BF16), wri_ref[k])
        r = jax.nn.sigmoid(ri[:, :LANES] + br_ref[:, cols])
        i_gate = jax.nn.sigmoid(ri[:, LANES:] + bi_ref[:, cols])
        log_a = r * neg_c_sp[:, cols]
        a = jnp.exp(log_a)
        one_minus_a2 = -jnp.tanh(log_a) * (1.0 + a * a)
        mult = jnp.where(one_minus_a2 > 0.0, one_minus_a2 * lax.rsqrt(one_minus_a2), 0.0)
        b = mult * i_gate * xc
        yield
        a_blk = [a[i * SCAN_BLOCK:(i + 1) * SCAN_BLOCK] for i in range(ts // SCAN_BLOCK)]
        b_blk = [b[i * SCAN_BLOCK:(i + 1) * SCAN_BLOCK] for i in range(ts // SCAN_BLOCK)]
        step = 1
        while step < SCAN_BLOCK:
            for i in range(len(a_blk)):
                r0 = i * SCAN_PITCH + SCAN_PAD
                scan_a[k, r0:r0 + SCAN_BLOCK, :] = a_blk[i]
                scan_b[k, r0:r0 + SCAN_BLOCK, :] = b_blk[i]
                b_blk[i] = a_blk[i] * scan_b[k, r0 - step:r0 - step + SCAN_BLOCK, :] + b_blk[i]
                a_blk[i] = a_blk[i] * scan_a[k, r0 - step:r0 - step + SCAN_BLOCK, :]
            step *= 2
        h_prev = lru_h_ref[0:1, cols]
        h_blk = []
        for a_i, b_i in zip(a_blk, b_blk):
            h_i = a_i * h_prev + b_i
            h_prev = h_i[SCAN_BLOCK - 1:SCAN_BLOCK, :]
            h_blk.append(h_i)
        lru_h_ref[0:1, cols] = h_prev
        h_lru = jnp.concatenate(h_blk, axis=0)
        yield
        y_b_pair.append((h_lru * jax.nn.gelu(proj.tile("ly", k), approximate=True)).astype(BF16))
        proj.pump()
        if k % 2 == 1:
            p0 = (k // 2) * PROJ_PIECE
            proj_a.append(_dot(y_a, pa_ref[:, p0:p0 + PROJ_PIECE]))
            part = _dot(jnp.concatenate(y_b_pair, axis=1), pb_ref[p0:p0 + PROJ_PIECE, :])
            proj_b = part if proj_b is None else proj_b + part
            y_b_pair = []
        yield

    g_pre = jnp.concatenate([proj.tile("g", t) for t in range(2 * n_tiles)], axis=1)
    gates = jax.nn.sigmoid(g_pre + gate_b_ref[...])
    merged = gates[:, :D_MODEL] * jnp.concatenate(proj_a, axis=1) + gates[:, D_MODEL:] * proj_b
    yield
    out = _dot(merged.astype(BF16), wout_ref[...])
    o_ref[...] = x + _rmsnorm(out, post_w_ref[...])


def _ffn_kernel(x_ref, pre_w_ref, post_w_ref, wup_ref, conv_w_ref, conv_b_ref, wdown_ref,
                o_ref, conv_buf, act_ref):

    @pl.when(pl.program_id(1) == 0)
    def _():
        conv_buf[:, 0:SUBLANES, :] = jnp.zeros((2 * FFN_DIM // LANES, SUBLANES, LANES), F32)

    x = x_ref[...]
    hb = _rmsnorm(x, pre_w_ref[...]).astype(BF16)
    for jb in range(FFN_DIM // FFN_BLOCK):
        g0 = jb * FFN_BLOCK
        v0 = FFN_DIM + jb * FFN_BLOCK
        gate = _causal_conv(conv_buf, _dot(hb, wup_ref[:, g0:g0 + FFN_BLOCK]),
                            conv_w_ref, conv_b_ref, g0, FFN_CONV)
        val = _causal_conv(conv_buf, _dot(hb, wup_ref[:, v0:v0 + FFN_BLOCK]),
                           conv_w_ref, conv_b_ref, v0, FFN_CONV)
        act = [jax.nn.gelu(g, approximate=True) * v for g, v in zip(gate, val)]
        act_ref[:, g0:g0 + FFN_BLOCK] = jnp.concatenate(act, axis=1).astype(BF16)
    f = _dot(act_ref[...], wdown_ref[...])
    o_ref[...] = x + _rmsnorm(f, post_w_ref[...])


def _const_spec(shape):
    zeros = (0,) * len(shape)
    return pl.BlockSpec(shape, lambda b, t: zeros, pipeline_mode=pl.Buffered(1))


def _tile_spec(rows, tile):
    return pl.BlockSpec((rows, tile, D_MODEL), lambda b, t: (b, t, 0))


def _call(body, name, rows, tile, x, consts, scratch):
    batch, seq, _ = x.shape
    assert seq % tile == 0 and batch % (rows or 1) == 0
    return pl.pallas_call(
        body,
        out_shape=jax.ShapeDtypeStruct(x.shape, F32),
        grid=(batch // (rows or 1), seq // tile),
        in_specs=[_tile_spec(rows, tile)] + [_const_spec(c.shape) for c in consts],
        out_specs=_tile_spec(rows, tile),
        scratch_shapes=scratch,
        compiler_params=pltpu.CompilerParams(
            dimension_semantics=("arbitrary", "arbitrary"),
            vmem_limit_bytes=VMEM_LIMIT),
        name=name,
    )(x, *consts)


def _row(v):
    return v.reshape(1, -1).astype(F32)


def _pad_lanes(v):
    return jnp.pad(v, [(0, 0)] * (v.ndim - 1) + [(0, LANES - v.shape[-1])])


def _pair_block_diag(w):
    w = w.reshape(LRU_BLOCKS // 2, 2, LRU_BLOCK, LRU_BLOCK)
    zero = jnp.zeros_like(w[:, 0])
    top = jnp.concatenate([w[:, 0], zero], axis=2)
    bot = jnp.concatenate([zero, w[:, 1]], axis=2)
    return jnp.concatenate([top, bot], axis=1)


def kernel(x, mix_pre_norm, mix_post_norm, w_in, ssm_conv_w, ssm_conv_b, ssm_dt_bias, ssm_a_log, ssm_d, ssm_norm, w_proj_ssm, lru_conv_w, lru_conv_b, lru_wr, lru_br, lru_wi, lru_bi, lru_lambda, w_proj_lru, gate_b, w_out, ffn_pre_norm, ffn_post_norm, w_ffn_up, ffn_conv_w, ffn_conv_b, w_ffn_down):
    assert SEQ_TILE % SSD_CHUNK == 0
    assert mix_pre_norm.shape[0] == 1, "one layer"
    assert w_in.shape[2] == W_IN_B0 + B_END
    l = 0
    head_of_lane = np.arange(SSM_INNER) // SSM_HEAD_DIM
    piece_row = np.arange(LANES)[:, None]
    expand = (piece_row % SSM_HEADS == head_of_lane[None, :]) & (piece_row < 3 * SSM_HEADS)
    wri = jnp.concatenate([_pair_block_diag(lru_wr[l]), _pair_block_diag(lru_wi[l])], axis=2)

    mixer_consts = [
        _row(mix_pre_norm[l]), _row(mix_post_norm[l]),
        w_in[l][:, :A_END].astype(BF16), w_in[l][:, W_IN_B0:].astype(BF16),
        ssm_conv_w[l].astype(F32), _row(ssm_conv_b[l]),
        _pad_lanes(_row(ssm_dt_bias[l])), _pad_lanes(_row(ssm_a_log[l])),
        _row(jnp.repeat(ssm_d[l], SSM_HEAD_DIM)), _row(ssm_norm[l]),
        jnp.asarray(expand, BF16), w_proj_ssm[l].astype(BF16),
        lru_conv_w[l].astype(F32), _row(lru_conv_b[l]), wri.astype(BF16),
        _row(lru_br[l]), _row(lru_bi[l]), _row(lru_lambda[l]), w_proj_lru[l].astype(BF16),
        _row(gate_b[l]), w_out[l].astype(BF16),
    ]
    mixer_scratch = [pltpu.VMEM((MIXER_STREAMS,) + shape, F32) for shape in (
        (SSM_CONV_DIM // LANES, SUBLANES + SEQ_TILE, LANES),
        (LRU_WIDTH // LANES, SUBLANES + SEQ_TILE, LANES),
        (SSM_STATE, SSM_INNER),
        (SUBLANES, LRU_WIDTH),
        (LRU_WIDTH // LANES, SEQ_TILE // SCAN_BLOCK * SCAN_PITCH, LANES),
        (LRU_WIDTH // LANES, SEQ_TILE // SCAN_BLOCK * SCAN_PITCH, LANES),
        (SEQ_TILE // SSD_CHUNK, CS_PAD + SSD_CHUNK, LANES),
    )]
    assert len(mixer_scratch) == N_MIXER_SCRATCH
    x1 = _call(_mixer_kernel, "mixer", MIXER_STREAMS, SEQ_TILE, x, mixer_consts, mixer_scratch)

    ffn_consts = [
        _row(ffn_pre_norm[l]), _row(ffn_post_norm[l]),
        w_ffn_up[l].astype(BF16), ffn_conv_w[l].astype(F32), _row(ffn_conv_b[l]),
        w_ffn_down[l].astype(BF16),
    ]
    ffn_scratch = [
        pltpu.VMEM((2 * FFN_DIM // LANES, SUBLANES + FFN_TILE, LANES), F32),
        pltpu.VMEM((FFN_TILE, FFN_DIM), BF16),
    ]
    return _call(_ffn_kernel, "ffn", None, FFN_TILE, x1, ffn_consts, ffn_scratch)
```

```python
import jax
import jax.numpy as jnp
import numpy as np
from jax import lax
from jax.experimental import pallas as pl
from jax.experimental.pallas import tpu as pltpu

F32 = jnp.float32
BF16 = jnp.bfloat16

D_MODEL = 1024
SSM_INNER = 1024
SSM_HEAD_DIM = 64
SSM_HEADS = 16
SSM_GROUPS = 2
SSM_STATE = 128
SSM_CONV = 4
SSM_CONV_DIM = SSM_INNER + 2 * SSM_GROUPS * SSM_STATE
LRU_WIDTH = 1024
LRU_BLOCKS = 16
LRU_BLOCK = 64
LRU_CONV = 4
LRU_C = 8.0
FFN_DIM = 3072
FFN_CONV = 3
RMS_EPS = 1e-6

LANES = 128
SUBLANES = 8
HEADS_PER_GROUP = SSM_HEADS // SSM_GROUPS
GROUP_WIDTH = SSM_INNER // SSM_GROUPS

A_Z = 0
A_XBC = A_Z + SSM_INNER
A_DT = A_XBC + SSM_CONV_DIM
A_END = A_DT + LANES
W_IN_B0 = A_DT + SSM_HEADS
B_LY = 0
B_LX = B_LY + LRU_WIDTH
B_G = B_LX + LRU_WIDTH
B_END = B_G + 2 * D_MODEL

MIXER_STREAMS = 2
STREAM_LEAD = 24
N_MIXER_SCRATCH = 7
SEQ_TILE = 256
FFN_TILE = 512
SSD_CHUNK = 128
SCAN_BLOCK = 8
SCAN_PAD = max(SUBLANES, SCAN_BLOCK // 2)
SCAN_PITCH = SCAN_PAD + SCAN_BLOCK
CS_PAD = SSD_CHUNK // 2
PROJ_PIECE = 256
FFN_BLOCK = 512
VMEM_LIMIT = 56 * 1024 * 1024


def _rmsnorm(x, w):
    return x * lax.rsqrt(jnp.mean(x * x, axis=-1, keepdims=True) + RMS_EPS) * w


def _softplus(x):
    return jnp.maximum(x, 0.0) + jnp.log1p(jnp.exp(-jnp.abs(x)))


def _dot(a, b):
    return jnp.dot(a, b, preferred_element_type=F32)


def _causal_conv(buf_ref, x_new, w_ref, b_ref, col0, taps, after_tile=None):
    rows = x_new.shape[0]
    outs = []
    for j in range(x_new.shape[1] // LANES):
        tile = col0 // LANES + j
        cols = slice(col0 + j * LANES, col0 + (j + 1) * LANES)
        xj = x_new[:, j * LANES:(j + 1) * LANES]
        buf_ref[tile, SUBLANES:SUBLANES + rows, :] = xj
        acc = b_ref[:, cols] + w_ref[taps - 1:taps, cols] * xj
        for t in range(taps - 1):
            back = taps - 1 - t
            acc = acc + w_ref[t:t + 1, cols] * buf_ref[tile, SUBLANES - back:SUBLANES - back + rows, :]
        buf_ref[tile, 0:SUBLANES, :] = xj[rows - SUBLANES:rows, :]
        outs.append(acc)
        if after_tile is not None:
            after_tile(j)
    return outs


def _expand_heads(v, expand_ref):
    lane = lax.broadcasted_iota(jnp.int32, v.shape, 1)
    v = jnp.where(lane < SSM_HEADS, v, 0.0)
    hi = v.astype(BF16).astype(F32)
    rem = v - hi
    mid = rem.astype(BF16).astype(F32)
    lo = rem - mid
    packed = hi + pltpu.roll(mid, SSM_HEADS, 1) + pltpu.roll(lo, 2 * SSM_HEADS, 1)
    return _dot(packed.astype(BF16), expand_ref[...])


class _ProjectionQueue:
    def __init__(self, hb):
        self._hb = hb
        self._pending = []
        self._done = {}

    def add(self, name, w_ref, col0, width):
        for j in range(width // PROJ_PIECE):
            self._pending.append((name, j, w_ref, col0 + j * PROJ_PIECE))

    def pump(self, count=1):
        for _ in range(min(count, len(self._pending))):
            name, j, w_ref, c0 = self._pending.pop(0)
            self._done[name, j] = _dot(self._hb, w_ref[:, c0:c0 + PROJ_PIECE])

    def tile(self, name, t):
        per_piece = PROJ_PIECE // LANES
        while (name, t // per_piece) not in self._done:
            self.pump()
        off = (t % per_piece) * LANES
        return self._done[name, t // per_piece][:, off:off + LANES]


def _run_interleaved(streams, lead):
    live = list(streams)
    for _ in range(lead):
        next(live[0], None)
    while live:
        live = [g for g in live if next(g, _DONE) is not _DONE]


_DONE = object()


def _mixer_kernel(x_ref, *refs):
    n_consts = len(refs) - 1 - N_MIXER_SCRATCH
    consts = refs[:n_consts]
    o_ref = refs[n_consts]
    scratch = refs[n_consts + 1:]
    sconv_buf, lconv_buf, state_ref, lru_h_ref, scan_a, scan_b, cs_buf = scratch

    @pl.when(pl.program_id(1) == 0)
    def _():
        sconv_buf[:, :, 0:SUBLANES, :] = jnp.zeros((MIXER_STREAMS, SSM_CONV_DIM // LANES, SUBLANES, LANES), F32)
        lconv_buf[:, :, 0:SUBLANES, :] = jnp.zeros((MIXER_STREAMS, LRU_WIDTH // LANES, SUBLANES, LANES), F32)
        state_ref[...] = jnp.zeros_like(state_ref)
        lru_h_ref[...] = jnp.zeros_like(lru_h_ref)
        scan_a[...] = jnp.ones_like(scan_a)
        scan_b[...] = jnp.zeros_like(scan_b)
        cs_buf[...] = jnp.zeros_like(cs_buf)

    streams = [_mixer_stream(x_ref.at[s], o_ref.at[s], consts, [r.at[s] for r in scratch])
               for s in range(MIXER_STREAMS)]
    _run_interleaved(streams, STREAM_LEAD)


def _mixer_stream(x_ref, o_ref, consts, scratch):
    (pre_w_ref, post_w_ref, wa_ref, wb_ref,
     sconv_w_ref, sconv_b_ref, dt_bias_ref, a_log_ref, d_skip_ref, snorm_w_ref,
     expand_ref, pa_ref,
     lconv_w_ref, lconv_b_ref, wri_ref, br_ref, bi_ref, lam_ref, pb_ref,
     gate_b_ref, wout_ref) = consts
    sconv_buf, lconv_buf, state_ref, lru_h_ref, scan_a, scan_b, cs_buf = scratch
    ts = SEQ_TILE
    n_tiles = D_MODEL // LANES

    x = x_ref[...]
    hb = _rmsnorm(x, pre_w_ref[...]).astype(BF16)
    yield
    proj = _ProjectionQueue(hb)
    proj.add("lx", wb_ref, B_LX, LRU_WIDTH)
    proj.add("z", wa_ref, A_Z, SSM_INNER)
    proj.add("ly", wb_ref, B_LY, LRU_WIDTH)
    proj.add("g", wb_ref, B_G, 2 * D_MODEL)

    xbc_tiles = _causal_conv(sconv_buf, _dot(hb, wa_ref[:, A_XBC:A_DT]), sconv_w_ref, sconv_b_ref,
                             0, SSM_CONV, after_tile=lambda j: proj.pump() if j % 3 == 2 else None)
    xbc = jnp.concatenate([v * jax.nn.sigmoid(v) for v in xbc_tiles], axis=1)
    xs = xbc[:, :SSM_INNER]
    bm = xbc[:, SSM_INNER:SSM_INNER + SSM_GROUPS * SSM_STATE]
    cm = xbc[:, SSM_INNER + SSM_GROUPS * SSM_STATE:]
    yield
    dt = _softplus(_dot(hb, wa_ref[:, A_DT:A_END]) + dt_bias_ref[...])
    a_dt = dt * -jnp.exp(a_log_ref[...])

    lc = SSD_CHUNK
    row_i = lax.broadcasted_iota(jnp.int32, (lc, lc), 0)
    col_i = lax.broadcasted_iota(jnp.int32, (lc, lc), 1)
    causal = row_i >= col_i
    lane = lax.broadcasted_iota(jnp.int32, (lc, LANES), 1)
    lo_half = lane < SSM_HEAD_DIM

    y_chunks = []
    for c in range(ts // lc):
        rows = slice(c * lc, (c + 1) * lc)
        dt_c = dt[rows]
        a_cs = a_dt[rows]
        step = 1
        while step < lc:
            cs_buf[c, CS_PAD:CS_PAD + lc, :] = a_cs
            a_cs = a_cs + cs_buf[c, CS_PAD - step:CS_PAD - step + lc, :]
            step *= 2
        a_cs_t = a_cs.T
        dt_t = dt_c.T
        a_last = a_cs[lc - 1:lc, :]
        e_in = jnp.exp(a_cs)
        w_out = jnp.exp(a_last - a_cs) * dt_c
        both = _expand_heads(jnp.concatenate([e_in, w_out], axis=0), expand_ref)
        e_full = both[:lc]
        w_full = both[lc:]
        xs_c = xs[rows]
        xw = (w_full * xs_c).astype(BF16)
        yield

        y_parts = []
        for g in range(SSM_GROUPS):
            gs = slice(g * GROUP_WIDTH, (g + 1) * GROUP_WIDTH)
            b_g = bm[rows, g * SSM_STATE:(g + 1) * SSM_STATE]
            c_g = cm[rows, g * SSM_STATE:(g + 1) * SSM_STATE].astype(BF16)
            cb = lax.dot_general(c_g, b_g.astype(BF16), (((1,), (1,)), ((), ())),
                                 preferred_element_type=F32)
            yd = []
            for k in range(HEADS_PER_GROUP // 2):
                pair = g * (HEADS_PER_GROUP // 2) + k
                m_pair = []
                for h in (2 * pair, 2 * pair + 1):
                    seg = a_cs[:, h:h + 1] - a_cs_t[h:h + 1, :]
                    decay = jnp.exp(jnp.where(causal, seg, -jnp.inf))
                    m_pair.append((decay * cb * dt_t[h:h + 1, :]).astype(BF16))
                x_pair = xs_c[:, pair * LANES:(pair + 1) * LANES]
                rhs = jnp.concatenate([jnp.where(lo_half, x_pair, 0.0).astype(BF16),
                                       jnp.where(lo_half, 0.0, x_pair).astype(BF16)], axis=0)
                yd.append(_dot(jnp.concatenate(m_pair, axis=1), rhs))
                if (pair + c) % 2 == 0:
                    proj.pump()
                yield
            y_diag = jnp.concatenate(yd, axis=1)
            st = state_ref[:, gs]
            y_off = e_full[:, gs] * _dot(c_g, st.astype(BF16))
            upd = _dot(b_g.T.astype(BF16), xw[:, gs])
            state_ref[:, gs] = e_full[lc - 1:lc, gs] * st + upd
            y_parts.append(y_diag + y_off)
            yield
        y_chunks.append(jnp.concatenate(y_parts, axis=1))
    y = jnp.concatenate(y_chunks, axis=0) + d_skip_ref[...] * xs

    z = jnp.concatenate([proj.tile("z", t) for t in range(n_tiles)], axis=1)
    yg = y * (z * jax.nn.sigmoid(z))
    yn = []
    for g in range(SSM_GROUPS):
        part = yg[:, g * GROUP_WIDTH:(g + 1) * GROUP_WIDTH]
        yn.append(part * lax.rsqrt(jnp.mean(part * part, axis=-1, keepdims=True) + RMS_EPS))
    y_a = (jnp.concatenate(yn, axis=1) * snorm_w_ref[...]).astype(BF16)
    yield

    lx = jnp.concatenate([proj.tile("lx", t) for t in range(n_tiles)], axis=1)
    xc_tiles = _causal_conv(lconv_buf, lx, lconv_w_ref, lconv_b_ref, 0, LRU_CONV,
                            after_tile=lambda j: proj.pump() if j % 4 == 3 else None)
    neg_c_sp = (-LRU_C) * _softplus(-lam_ref[...])
    yield
    proj_a = []
    proj_b = None
    y_b_pair = []
    for k, xc in enumerate(xc_tiles):
        cols = slice(k * LANES, (k + 1) * LANES)
        ri = _dot(xc.astype(BF16), wri_ref[k])
        r = jax.nn.sigmoid(ri[:, :LANES] + br_ref[:, cols])
        i_gate = jax.nn.sigmoid(ri[:, LANES:] + bi_ref[:, cols])
        log_a = r * neg_c_sp[:, cols]
        a = jnp.exp(log_a)
        one_minus_a2 = -jnp.tanh(log_a) * (1.0 + a * a)
        mult = jnp.where(one_minus_a2 > 0.0, one_minus_a2 * lax.rsqrt(one_minus_a2), 0.0)
        b = mult * i_gate * xc
        yield
        a_blk = [a[i * SCAN_BLOCK:(i + 1) * SCAN_BLOCK] for i in range(ts // SCAN_BLOCK)]
        b_blk = [b[i * SCAN_BLOCK:(i + 1) * SCAN_BLOCK] for i in range(ts // SCAN_BLOCK)]
        step = 1
        while step < SCAN_BLOCK:
            for i in range(len(a_blk)):
                r0 = i * SCAN_PITCH + SCAN_PAD
                scan_a[k, r0:r0 + SCAN_BLOCK, :] = a_blk[i]
                scan_b[k, r0:r0 + SCAN_BLOCK, :] = b_blk[i]
                b_blk[i] = a_blk[i] * scan_b[k, r0 - step:r0 - step + SCAN_BLOCK, :] + b_blk[i]
                a_blk[i] = a_blk[i] * scan_a[k, r0 - step:r0 - step + SCAN_BLOCK, :]
            step *= 2
        h_prev = lru_h_ref[0:1, cols]
        h_blk = []
        for a_i, b_i in zip(a_blk, b_blk):
            h_i = a_i * h_prev + b_i
            h_prev = h_i[SCAN_BLOCK - 1:SCAN_BLOCK, :]
            h_blk.append(h_i)
        lru_h_ref[0:1, cols] = h_prev
        h_lru = jnp.concatenate(h_blk, axis=0)
        yield
        y_b_pair.append((h_lru * jax.nn.gelu(proj.tile("ly", k), approximate=True)).astype(BF16))
        proj.pump()
        if k % 2 == 1:
            p0 = (k - 1) * LANES
            proj_a.append(_dot(y_a, pa_ref[:, p0:p0 + 2 * LANES]))
            part = _dot(jnp.concatenate(y_b_pair, axis=1), pb_ref[p0:p0 + 2 * LANES, :])
            proj_b = part if proj_b is None else proj_b + part
            y_b_pair = []
        yield

    g_pre = jnp.concatenate([proj.tile("g", t) for t in range(2 * n_tiles)], axis=1)
    gates = jax.nn.sigmoid(g_pre + gate_b_ref[...])
    merged = gates[:, :D_MODEL] * jnp.concatenate(proj_a, axis=1) + gates[:, D_MODEL:] * proj_b
    yield
    out = _dot(merged.astype(BF16), wout_ref[...])
    o_ref[...] = x + _rmsnorm(out, post_w_ref[...])


def _ffn_kernel(x_ref, pre_w_ref, post_w_ref, wup_ref, conv_w_ref, conv_b_ref, wdown_ref,
                o_ref, conv_buf, act_ref):

    @pl.when(pl.program_id(1) == 0)
    def _():
        conv_buf[:, 0:SUBLANES, :] = jnp.zeros((2 * FFN_DIM // LANES, SUBLANES, LANES), F32)

    x = x_ref[...]
    hb = _rmsnorm(x, pre_w_ref[...]).astype(BF16)
    for jb in range(FFN_DIM // FFN_BLOCK):
        g0 = jb * FFN_BLOCK
        v0 = FFN_DIM + jb * FFN_BLOCK
        gate = _causal_conv(conv_buf, _dot(hb, wup_ref[:, g0:g0 + FFN_BLOCK]),
                            conv_w_ref, conv_b_ref, g0, FFN_CONV)
        val = _causal_conv(conv_buf, _dot(hb, wup_ref[:, v0:v0 + FFN_BLOCK]),
                           conv_w_ref, conv_b_ref, v0, FFN_CONV)
        act = [jax.nn.gelu(g, approximate=True) * v for g, v in zip(gate, val)]
        act_ref[:, g0:g0 + FFN_BLOCK] = jnp.concatenate(act, axis=1).astype(BF16)
    f = _dot(act_ref[...], wdown_ref[...])
    o_ref[...] = x + _rmsnorm(f, post_w_ref[...])


def _const_spec(shape):
    zeros = (0,) * len(shape)
    return pl.BlockSpec(shape, lambda b, t: zeros, pipeline_mode=pl.Buffered(1))


def _tile_spec(rows, tile):
    return pl.BlockSpec((rows, tile, D_MODEL), lambda b, t: (b, t, 0))


def _call(body, name, rows, tile, x, consts, scratch):
    batch, seq, _ = x.shape
    assert seq % tile == 0 and batch % (rows or 1) == 0
    return pl.pallas_call(
        body,
        out_shape=jax.ShapeDtypeStruct(x.shape, F32),
        grid=(batch // (rows or 1), seq // tile),
        in_specs=[_tile_spec(rows, tile)] + [_const_spec(c.shape) for c in consts],
        out_specs=_tile_spec(rows, tile),
        scratch_shapes=scratch,
        compiler_params=pltpu.CompilerParams(
            dimension_semantics=("arbitrary", "arbitrary"),
            vmem_limit_bytes=VMEM_LIMIT),
        name=name,
    )(x, *consts)


def _row(v):
    return v.reshape(1, -1).astype(F32)


def _pad_lanes(v):
    return jnp.pad(v, [(0, 0)] * (v.ndim - 1) + [(0, LANES - v.shape[-1])])


def _pair_block_diag(w):
    w = w.reshape(LRU_BLOCKS // 2, 2, LRU_BLOCK, LRU_BLOCK)
    zero = jnp.zeros_like(w[:, 0])
    top = jnp.concatenate([w[:, 0], zero], axis=2)
    bot = jnp.concatenate([zero, w[:, 1]], axis=2)
    return jnp.concatenate([top, bot], axis=1)


def kernel(x, mix_pre_norm, mix_post_norm, w_in, ssm_conv_w, ssm_conv_b, ssm_dt_bias, ssm_a_log, ssm_d, ssm_norm, w_proj_ssm, lru_conv_w, lru_conv_b, lru_wr, lru_br, lru_wi, lru_bi, lru_lambda, w_proj_lru, gate_b, w_out, ffn_pre_norm, ffn_post_norm, w_ffn_up, ffn_conv_w, ffn_conv_b, w_ffn_down):
    assert SEQ_TILE % SSD_CHUNK == 0
    assert mix_pre_norm.shape[0] == 1, "one layer"
    assert w_in.shape[2] == W_IN_B0 + B_END
    l = 0
    head_of_lane = np.arange(SSM_INNER) // SSM_HEAD_DIM
    piece_row = np.arange(LANES)[:, None]
    expand = (piece_row % SSM_HEADS == head_of_lane[None, :]) & (piece_row < 3 * SSM_HEADS)
    wri = jnp.concatenate([_pair_block_diag(lru_wr[l]), _pair_block_diag(lru_wi[l])], axis=2)

    mixer_consts = [
        _row(mix_pre_norm[l]), _row(mix_post_norm[l]),
        w_in[l][:, :A_END].astype(BF16), w_in[l][:, W_IN_B0:].astype(BF16),
        ssm_conv_w[l].astype(F32), _row(ssm_conv_b[l]),
        _pad_lanes(_row(ssm_dt_bias[l])), _pad_lanes(_row(ssm_a_log[l])),
        _row(jnp.repeat(ssm_d[l], SSM_HEAD_DIM)), _row(ssm_norm[l]),
        jnp.asarray(expand, BF16), w_proj_ssm[l].astype(BF16),
        lru_conv_w[l].astype(F32), _row(lru_conv_b[l]), wri.astype(BF16),
        _row(lru_br[l]), _row(lru_bi[l]), _row(lru_lambda[l]), w_proj_lru[l].astype(BF16),
        _row(gate_b[l]), w_out[l].astype(BF16),
    ]
    mixer_scratch = [pltpu.VMEM((MIXER_STREAMS,) + shape, F32) for shape in (
        (SSM_CONV_DIM // LANES, SUBLANES + SEQ_TILE, LANES),
        (LRU_WIDTH // LANES, SUBLANES + SEQ_TILE, LANES),
        (SSM_STATE, SSM_INNER),
        (SUBLANES, LRU_WIDTH),
        (LRU_WIDTH // LANES, SEQ_TILE // SCAN_BLOCK * SCAN_PITCH, LANES),
        (LRU_WIDTH // LANES, SEQ_TILE // SCAN_BLOCK * SCAN_PITCH, LANES),
        (SEQ_TILE // SSD_CHUNK, CS_PAD + SSD_CHUNK, LANES),
    )]
    assert len(mixer_scratch) == N_MIXER_SCRATCH
    x1 = _call(_mixer_kernel, "mixer", MIXER_STREAMS, SEQ_TILE, x, mixer_consts, mixer_scratch)

    ffn_consts = [
        _row(ffn_pre_norm[l]), _row(ffn_post_norm[l]),
        w_ffn_up[l].astype(BF16), ffn_conv_w[l].astype(F32), _row(ffn_conv_b[l]),
        w_ffn_down[l].astype(BF16),
    ]
    ffn_scratch = [
        pltpu.VMEM((2 * FFN_DIM // LANES, SUBLANES + FFN_TILE, LANES), F32),
        pltpu.VMEM((FFN_TILE, FFN_DIM), BF16),
    ]
    return _call(_ffn_kernel, "ffn", None, FFN_TILE, x1, ffn_consts, ffn_scratch)
```
